```python
import jax, jax.numpy as jnp
from jax import lax
import numpy as np

D_MODEL = 1024
BATCH = 16
SEQ = 2048
DEPTH = 2

CHUNK = 64
D_FF = 2816
CONV_W = D_MODEL
CONV_K = 3
CONV_GROUPS = 16
POOL_W = D_MODEL
POOL_WINDOWS = (2, 4, 8, 16)
POOL_GROUPS = len(POOL_WINDOWS)
POOL_GC = POOL_W // POOL_GROUPS
N_BRANCH = 2
IN_COLS = 3 * CONV_W + POOL_W + N_BRANCH * D_MODEL
EPS = 1e-6

kernel_name = "hybrid_conv_pool_gated_macaron"


def rms_norm(x, g):
    xf = x.astype(jnp.float32)
    y = xf * lax.rsqrt(jnp.mean(xf * xf, axis=-1, keepdims=True) + EPS)
    return (y * g.astype(jnp.float32)).astype(x.dtype)


def swiglu(h, w_gate_up, w_down):
    gu = h @ w_gate_up
    g, u = jnp.split(gu, 2, axis=-1)
    return (jax.nn.silu(g) * u) @ w_down


def causal_depthwise_conv(u, w):
    s = u.shape[1]
    u_pad = jnp.pad(u, ((0, 0), (CONV_K - 1, 0), (0, 0)))
    y = w[0] * u_pad[:, 0:s]
    for k in range(1, CONV_K):
        y = y + w[k] * u_pad[:, k:k + s]
    return y


def trailing_mean(u, win):
    s = u.shape[1]
    cs = jnp.cumsum(u, axis=1)
    cs_shift = jnp.pad(cs, ((0, 0), (win, 0), (0, 0)))[:, :s]
    count = jnp.minimum(jnp.arange(1, s + 1), win).astype(jnp.float32)
    return (cs - cs_shift) / count[None, :, None]


def pool_mixer(p, w_pg, b_pg, scale):
    b, s, _ = p.shape
    pf = p.astype(jnp.float32).reshape(b, s, POOL_GROUPS, POOL_GC)
    pooled = jnp.stack([trailing_mean(pf[:, :, gi], win) for gi, win in enumerate(POOL_WINDOWS)], axis=2)
    d = (pooled - pf).astype(p.dtype)
    y = jnp.einsum('bsgc,gcd->bsgd', d, w_pg) + b_pg
    return y.reshape(b, s, POOL_W) * scale


def hybrid_mixer(h, w_in, b_gate, conv_w, w_conv_out, w_pg, b_pg, pool_scale, w_pool_out, w_o):
    z = h @ w_in
    b_g, c_g, v, p, gates = jnp.split(
        z, [CONV_W, 2 * CONV_W, 3 * CONV_W, 3 * CONV_W + POOL_W], axis=-1)
    y_a = (b_g * causal_depthwise_conv(c_g * v, conv_w)) @ w_conv_out
    y_p = pool_mixer(p, w_pg, b_pg, pool_scale) @ w_pool_out
    g = jax.nn.sigmoid((gates + b_gate).astype(jnp.float32)).astype(h.dtype)
    g_a, g_p = jnp.split(g, N_BRANCH, axis=-1)
    return (g_a * y_a + g_p * y_p) @ w_o


def setup_inputs(seed: int = 0) -> dict:
    key = jax.random.key(seed)
    ks = jax.random.split(key, 24)
    f32 = jnp.float32

    def nrm(k, shape, fan_in):
        return jax.random.normal(k, shape, f32) * (fan_in ** -0.5)

    def gain(k, shape):
        return 1.0 + 0.05 * jax.random.normal(k, shape, f32)

    L = DEPTH
    return {
        "x": jax.random.normal(ks[0], (BATCH, SEQ, D_MODEL), f32),
        "ffn1_pre": gain(ks[1], (L, D_MODEL)),
        "ffn1_post": gain(ks[2], (L, D_MODEL)),
        "ffn1_w_gate_up": nrm(ks[3], (L, D_MODEL, 2 * D_FF), D_MODEL),
        "ffn1_w_down": nrm(ks[4], (L, D_FF, D_MODEL), D_FF),
        "mix_pre": gain(ks[5], (L, D_MODEL)),
        "mix_post": gain(ks[6], (L, D_MODEL)),
        "w_in": nrm(ks[7], (L, D_MODEL, IN_COLS), D_MODEL),
        "b_gate": 0.01 * jax.random.normal(ks[8], (L, N_BRANCH * D_MODEL), f32),
        "conv_w": nrm(ks[9], (L, CONV_K, CONV_W), CONV_K),
        "w_conv_out": nrm(ks[10], (L, CONV_W, D_MODEL), CONV_W),
        "w_pool_group": nrm(ks[11], (L, POOL_GROUPS, POOL_GC, POOL_GC), POOL_GC),
        "b_pool_group": 0.01 * jax.random.normal(ks[12], (L, POOL_GROUPS, POOL_GC), f32),
        "pool_scale": gain(ks[13], (L, POOL_W)),
        "w_pool_out": nrm(ks[14], (L, POOL_W, D_MODEL), POOL_W),
        "w_o": nrm(ks[15], (L, D_MODEL, D_MODEL), D_MODEL),
        "ffn2_pre": gain(ks[16], (L, D_MODEL)),
        "ffn2_post": gain(ks[17], (L, D_MODEL)),
        "ffn2_w_gate_up": nrm(ks[18], (L, D_MODEL, 2 * D_FF), D_MODEL),
        "ffn2_w_down": nrm(ks[19], (L, D_FF, D_MODEL), D_FF),
    }


def reference(x, ffn1_pre, ffn1_post, ffn1_w_gate_up, ffn1_w_down,
              mix_pre, mix_post, w_in, b_gate, conv_w, w_conv_out,
              w_pool_group, b_pool_group, pool_scale, w_pool_out, w_o,
              ffn2_pre, ffn2_post, ffn2_w_gate_up, ffn2_w_down):
    for l in range(DEPTH):
        h = swiglu(rms_norm(x, ffn1_pre[l]), ffn1_w_gate_up[l], ffn1_w_down[l])
        x = x + 0.5 * rms_norm(h, ffn1_post[l])
        h = hybrid_mixer(rms_norm(x, mix_pre[l]), w_in[l], b_gate[l], conv_w[l],
                         w_conv_out[l], w_pool_group[l], b_pool_group[l],
                         pool_scale[l], w_pool_out[l], w_o[l])
        x = x + rms_norm(h, mix_post[l])
        h = swiglu(rms_norm(x, ffn2_pre[l]), ffn2_w_gate_up[l], ffn2_w_down[l])
        x = x + 0.5 * rms_norm(h, ffn2_post[l])
    return x
```

```python
import functools

import jax
import jax.numpy as jnp
from jax import lax
from jax.experimental import pallas as pl
from jax.experimental.pallas import tpu as pltpu

D_MODEL = 1024
SEQ = 2048
D_FF = 2816
CONV_K = 3
POOL_WINDOWS = (2, 4, 8, 16)
POOL_GC = D_MODEL // len(POOL_WINDOWS)
EPS = 1e-6

F32 = jnp.float32
BF16 = jnp.bfloat16

FFN_TM = 512
MIX_TM = 512
FF_CHUNK = 256
CONV_HALO = 8
POOL_HALO = 16

VMEM_LIMIT_BYTES = 56 * 1024 * 1024


def _rms_norm(x, g):
    y = x * lax.rsqrt(jnp.mean(x * x, axis=-1, keepdims=True) + EPS)
    return y * g


def _dot(a, b):
    return jnp.dot(a, b, preferred_element_type=F32)


def _resident(shape):
    zeros = (0,) * len(shape)
    return pl.BlockSpec(shape, lambda i: zeros, pipeline_mode=pl.Buffered(1))


def _ffn_kernel(x_ref, pre_ref, post_ref, wgu_ref, wd_ref, o_ref, act_ref):
    x = x_ref[...]
    h = _rms_norm(x, pre_ref[...]).astype(BF16)
    for c0 in range(0, D_FF, FF_CHUNK):
        g = _dot(h, wgu_ref[:, c0:c0 + FF_CHUNK])
        u = _dot(h, wgu_ref[:, D_FF + c0:D_FF + c0 + FF_CHUNK])
        act_ref[:, c0:c0 + FF_CHUNK] = (g * jax.nn.sigmoid(g) * u).astype(BF16)
    y = _dot(act_ref[...], wd_ref[...])
    o_ref[...] = x + 0.5 * _rms_norm(y, post_ref[...])


def _ffn(x, pre, post, wgu, wd):
    t = x.shape[0]
    row_spec = pl.BlockSpec((FFN_TM, D_MODEL), lambda i: (i, 0))
    return pl.pallas_call(
        _ffn_kernel,
        grid=(t // FFN_TM,),
        in_specs=[
            row_spec,
            _resident((1, D_MODEL)),
            _resident((1, D_MODEL)),
            _resident((D_MODEL, 2 * D_FF)),
            _resident((D_FF, D_MODEL)),
        ],
        out_specs=row_spec,
        out_shape=jax.ShapeDtypeStruct(x.shape, x.dtype),
        scratch_shapes=[pltpu.VMEM((FFN_TM, D_FF), BF16)],
        compiler_params=pltpu.CompilerParams(
            dimension_semantics=("arbitrary",), vmem_limit_bytes=VMEM_LIMIT_BYTES),
        name="ffn",
    )(x, pre, post, wgu, wd)


def _mixer_kernel(x_ref, pre_ref, post_ref, win_ref, bgate_ref, convw_ref, wco_ref,
                  wpg_ref, bpg_ref, pscale_ref, wpo_ref, wo_ref, o_ref,
                  cv_ref, p_ref, yp_ref):
    tm = MIX_TM
    tile_in_seq = lax.rem(pl.program_id(0), SEQ // tm)

    @pl.when(tile_in_seq == 0)
    def _():
        cv_ref[0:CONV_HALO, :] = jnp.zeros((CONV_HALO, D_MODEL), F32)
        p_ref[0:POOL_HALO, :] = jnp.zeros((POOL_HALO, D_MODEL), F32)

    @pl.when(tile_in_seq != 0)
    def _():
        cv_ref[0:CONV_HALO, :] = cv_ref[tm:tm + CONV_HALO, :]
        p_ref[0:POOL_HALO, :] = p_ref[tm:tm + POOL_HALO, :]

    x = x_ref[...]
    h = _rms_norm(x, pre_ref[...]).astype(BF16)

    def proj(k):
        return _dot(h, win_ref[:, k * D_MODEL:(k + 1) * D_MODEL])

    cv_ref[CONV_HALO:CONV_HALO + tm, :] = proj(1) * proj(2)
    cw = convw_ref[...]
    y = cw[0:1, :] * cv_ref[CONV_HALO - 2:CONV_HALO - 2 + tm, :]
    y = y + cw[1:2, :] * cv_ref[CONV_HALO - 1:CONV_HALO - 1 + tm, :]
    y = y + cw[2:3, :] * cv_ref[CONV_HALO:CONV_HALO + tm, :]
    y_a = _dot((proj(0) * y).astype(BF16), wco_ref[...])
    g_a = jax.nn.sigmoid(proj(4) + bgate_ref[:, 0:D_MODEL])
    mixed = g_a * y_a

    p_ref[POOL_HALO:POOL_HALO + tm, :] = proj(3)
    t_seq = tile_in_seq * tm + lax.broadcasted_iota(jnp.int32, (tm, 1), 0)
    for gi, win in enumerate(POOL_WINDOWS):
        cols = slice(gi * POOL_GC, (gi + 1) * POOL_GC)
        cur = p_ref[POOL_HALO:POOL_HALO + tm, cols]
        s = cur
        for k in range(1, win):
            s = s + p_ref[POOL_HALO - k:POOL_HALO - k + tm, cols]
        inv_count = 1.0 / jnp.minimum(t_seq + 1, win).astype(F32)
        d = (s * inv_count - cur).astype(BF16)
        yg = _dot(d, wpg_ref[gi]) + bpg_ref[:, cols]
        yp_ref[:, cols] = (yg * pscale_ref[:, cols]).astype(BF16)
    y_p = _dot(yp_ref[...], wpo_ref[...])
    g_p = jax.nn.sigmoid(proj(5) + bgate_ref[:, D_MODEL:2 * D_MODEL])
    mixed = mixed + g_p * y_p

    out = _dot(mixed.astype(BF16), wo_ref[...])
    o_ref[...] = x + _rms_norm(out, post_ref[...])


def _mixer(x, pre, post, win, bgate, convw, wco, wpg, bpg, pscale, wpo, wo):
    t = x.shape[0]
    tm = MIX_TM
    row_spec = pl.BlockSpec((tm, D_MODEL), lambda i: (i, 0))
    n_groups = len(POOL_WINDOWS)
    return pl.pallas_call(
        _mixer_kernel,
        grid=(t // tm,),
        in_specs=[
            row_spec,
            _resident((1, D_MODEL)),
            _resident((1, D_MODEL)),
            _resident((D_MODEL, 6 * D_MODEL)),
            _resident((1, 2 * D_MODEL)),
            _resident((CONV_K, D_MODEL)),
            _resident((D_MODEL, D_MODEL)),
            _resident((n_groups, POOL_GC, POOL_GC)),
            _resident((1, D_MODEL)),
            _resident((1, D_MODEL)),
            _resident((D_MODEL, D_MODEL)),
            _resident((D_MODEL, D_MODEL)),
        ],
        out_specs=row_spec,
        out_shape=jax.ShapeDtypeStruct(x.shape, x.dtype),
        scratch_shapes=[
            pltpu.VMEM((CONV_HALO + tm, D_MODEL), F32),
            pltpu.VMEM((POOL_HALO + tm, D_MODEL), F32),
            pltpu.VMEM((tm, D_MODEL), BF16),
        ],
        compiler_params=pltpu.CompilerParams(
            dimension_semantics=("arbitrary",), vmem_limit_bytes=VMEM_LIMIT_BYTES),
        name="mixer",
    )(x, pre, post, win, bgate, convw, wco, wpg, bpg, pscale, wpo, wo)


def kernel(x, ffn1_pre, ffn1_post, ffn1_w_gate_up, ffn1_w_down, mix_pre, mix_post, w_in, b_gate,
           conv_w, w_conv_out, w_pool_group, b_pool_group, pool_scale, w_pool_out, w_o,
           ffn2_pre, ffn2_post, ffn2_w_gate_up, ffn2_w_down):
    b, s, d = x.shape
    assert (s, d) == (SEQ, D_MODEL) and (b * s) % FFN_TM == 0 and SEQ % MIX_TM == 0
    depth = w_in.shape[0]
    xf = x.reshape(b * s, d)

    def row(v):
        return v.reshape(1, -1)

    for l in range(depth):
        xf = _ffn(xf, row(ffn1_pre[l]), row(ffn1_post[l]),
                  ffn1_w_gate_up[l].astype(BF16), ffn1_w_down[l].astype(BF16))
        xf = _mixer(xf, row(mix_pre[l]), row(mix_post[l]), w_in[l].astype(BF16), row(b_gate[l]),
                    conv_w[l], w_conv_out[l].astype(BF16), w_pool_group[l].astype(BF16),
                    row(b_pool_group[l]), row(pool_scale[l]), w_pool_out[l].astype(BF16),
                    w_o[l].astype(BF16))
        xf = _ffn(xf, row(ffn2_pre[l]), row(ffn2_post[l]),
                  ffn2_w_gate_up[l].astype(BF16), ffn2_w_down[l].astype(BF16))
    return xf.reshape(b, s, d)
```

```python
import jax
import jax.numpy as jnp
from jax import lax
from jax.experimental import pallas as pl
from jax.experimental.pallas import tpu as pltpu

D_MODEL = 1024
SEQ = 2048
D_FF = 2816
CONV_K = 3
POOL_WINDOWS = (2, 4, 8, 16)
POOL_GC = D_MODEL // len(POOL_WINDOWS)
EPS = 1e-6

F32 = jnp.float32
BF16 = jnp.bfloat16

FFN_TM = 512
MIX_TM = 512
FF_CHUNK = 256
CONV_HALO = 8
POOL_HALO = 16

VMEM_LIMIT_BYTES = 60 * 1024 * 1024


def _rms_norm(x, g):
    y = x * lax.rsqrt(jnp.mean(x * x, axis=-1, keepdims=True) + EPS)
    return y * g


def _dot(a, b):
    return jnp.dot(a.astype(BF16), b.astype(BF16), preferred_element_type=F32)


def _layer_resident(layer, shape):
    zeros = (0,) * len(shape)
    return pl.BlockSpec((None,) + tuple(shape), lambda i: (layer,) + zeros,
                        pipeline_mode=pl.Buffered(1))


def _ffn_kernel(x_ref, pre_ref, post_ref, wgu_ref, wd_ref, o_ref, act_ref):
    x = x_ref[...]
    h = _rms_norm(x, pre_ref[...]).astype(BF16)
    for c0 in range(0, D_FF, FF_CHUNK):
        g = _dot(h, wgu_ref[:, c0:c0 + FF_CHUNK])
        u = _dot(h, wgu_ref[:, D_FF + c0:D_FF + c0 + FF_CHUNK])
        act_ref[:, c0:c0 + FF_CHUNK] = (g * jax.nn.sigmoid(g) * u).astype(BF16)
    y = _dot(act_ref[...], wd_ref[...])
    o_ref[...] = x + 0.5 * _rms_norm(y, post_ref[...])


def _ffn(layer, x, pre, post, wgu, wd):
    t = x.shape[0]
    row_spec = pl.BlockSpec((FFN_TM, D_MODEL), lambda i: (i, 0))
    return pl.pallas_call(
        _ffn_kernel,
        grid=(t // FFN_TM,),
        in_specs=[
            row_spec,
            _layer_resident(layer, (1, D_MODEL)),
            _layer_resident(layer, (1, D_MODEL)),
            _layer_resident(layer, (D_MODEL, 2 * D_FF)),
            _layer_resident(layer, (D_FF, D_MODEL)),
        ],
        out_specs=row_spec,
        out_shape=jax.ShapeDtypeStruct(x.shape, x.dtype),
        scratch_shapes=[pltpu.VMEM((FFN_TM, D_FF), BF16)],
        compiler_params=pltpu.CompilerParams(
            dimension_semantics=("arbitrary",), vmem_limit_bytes=VMEM_LIMIT_BYTES),
        name="ffn",
    )(x, pre, post, wgu, wd)


def _mixer_kernel(x_ref, pre_ref, post_ref, win_ref, bgate_ref, convw_ref, wco_ref,
                  wpg_ref, bpg_ref, pscale_ref, wpo_ref, wo_ref, o_ref,
                  cv_ref, p_ref, yp_ref):
    tm = MIX_TM
    tile_in_seq = lax.rem(pl.program_id(0), SEQ // tm)

    @pl.when(tile_in_seq == 0)
    def _():
        cv_ref[0:CONV_HALO, :] = jnp.zeros((CONV_HALO, D_MODEL), F32)
        p_ref[0:POOL_HALO, :] = jnp.zeros((POOL_HALO, D_MODEL), F32)

    @pl.when(tile_in_seq != 0)
    def _():
        cv_ref[0:CONV_HALO, :] = cv_ref[tm:tm + CONV_HALO, :]
        p_ref[0:POOL_HALO, :] = p_ref[tm:tm + POOL_HALO, :]

    x = x_ref[...]
    h = _rms_norm(x, pre_ref[...]).astype(BF16)

    def proj(k):
        return _dot(h, win_ref[:, k * D_MODEL:(k + 1) * D_MODEL])

    p_ref[POOL_HALO:POOL_HALO + tm, :] = proj(3)
    t_seq = tile_in_seq * tm + lax.broadcasted_iota(jnp.int32, (tm, 1), 0)
    for gi, win in enumerate(POOL_WINDOWS):
        cols = slice(gi * POOL_GC, (gi + 1) * POOL_GC)
        ext = p_ref[:, cols]
        s = ext
        k = 1
        while k < win:
            s = s + pltpu.roll(s, k, axis=0)
            k *= 2
        inv_count = 1.0 / jnp.minimum(t_seq + 1, win).astype(F32)
        d = s[POOL_HALO:, :] * inv_count - ext[POOL_HALO:, :]
        yg = _dot(d, wpg_ref[gi]) + bpg_ref[:, cols]
        yp_ref[:, cols] = (yg * pscale_ref[:, cols]).astype(BF16)
    y_p = _dot(yp_ref[...], wpo_ref[...])
    g_p = jax.nn.sigmoid(proj(5) + bgate_ref[:, D_MODEL:2 * D_MODEL])
    mixed = g_p * y_p

    cv_ref[CONV_HALO:CONV_HALO + tm, :] = proj(1) * proj(2)
    cw = convw_ref[...]
    cv = cv_ref[...]
    y = (cw[0:1, :] * pltpu.roll(cv, 2, axis=0) + cw[1:2, :] * pltpu.roll(cv, 1, axis=0)
         + cw[2:3, :] * cv)
    y_a = _dot(proj(0) * y[CONV_HALO:, :], wco_ref[...])
    g_a = jax.nn.sigmoid(proj(4) + bgate_ref[:, 0:D_MODEL])
    mixed = g_a * y_a + mixed

    out = _dot(mixed, wo_ref[...])
    o_ref[...] = x + _rms_norm(out, post_ref[...])


def _mixer(layer, x, pre, post, win, bgate, convw, wco, wpg, bpg, pscale, wpo, wo):
    t = x.shape[0]
    tm = MIX_TM
    row_spec = pl.BlockSpec((tm, D_MODEL), lambda i: (i, 0))
    n_groups = len(POOL_WINDOWS)
    return pl.pallas_call(
        _mixer_kernel,
        grid=(t // tm,),
        in_specs=[
            row_spec,
            _layer_resident(layer, (1, D_MODEL)),
            _layer_resident(layer, (1, D_MODEL)),
            _layer_resident(layer, (D_MODEL, 6 * D_MODEL)),
            _layer_resident(layer, (1, 2 * D_MODEL)),
            _layer_resident(layer, (CONV_K, D_MODEL)),
            _layer_resident(layer, (D_MODEL, D_MODEL)),
            _layer_resident(layer, (n_groups, POOL_GC, POOL_GC)),
            _layer_resident(layer, (1, D_MODEL)),
            _layer_resident(layer, (1, D_MODEL)),
            _layer_resident(layer, (D_MODEL, D_MODEL)),
            _layer_resident(layer, (D_MODEL, D_MODEL)),
        ],
        out_specs=row_spec,
        out_shape=jax.ShapeDtypeStruct(x.shape, x.dtype),
        scratch_shapes=[
            pltpu.VMEM((CONV_HALO + tm, D_MODEL), F32),
            pltpu.VMEM((POOL_HALO + tm, D_MODEL), F32),
            pltpu.VMEM((tm, D_MODEL), BF16),
        ],
        compiler_params=pltpu.CompilerParams(
            dimension_semantics=("arbitrary",), vmem_limit_bytes=VMEM_LIMIT_BYTES),
        name="mixer",
    )(x, pre, post, win, bgate, convw, wco, wpg, bpg, pscale, wpo, wo)


def kernel(x, ffn1_pre, ffn1_post, ffn1_w_gate_up, ffn1_w_down, mix_pre, mix_post, w_in, b_gate,
           conv_w, w_conv_out, w_pool_group, b_pool_group, pool_scale, w_pool_out, w_o,
           ffn2_pre, ffn2_post, ffn2_w_gate_up, ffn2_w_down):
    b, s, d = x.shape
    assert (s, d) == (SEQ, D_MODEL) and (b * s) % FFN_TM == 0 and SEQ % MIX_TM == 0
    depth = w_in.shape[0]
    xf = x.reshape(b * s, d)

    def rows(v):
        return v.reshape(depth, 1, -1)

    for l in range(depth):
        xf = _ffn(l, xf, rows(ffn1_pre), rows(ffn1_post), ffn1_w_gate_up, ffn1_w_down)
        xf = _mixer(l, xf, rows(mix_pre), rows(mix_post), w_in, rows(b_gate), conv_w, w_conv_out,
                    w_pool_group, rows(b_pool_group), rows(pool_scale), w_pool_out, w_o)
        xf = _ffn(l, xf, rows(ffn2_pre), rows(ffn2_post), ffn2_w_gate_up, ffn2_w_down)
    return xf.reshape(b, s, d)
```

```python
import jax
import jax.numpy as jnp
from jax import lax
from jax.experimental import pallas as pl
from jax.experimental.pallas import tpu as pltpu

D_MODEL = 1024
SEQ = 2048
D_FF = 2816
CONV_K = 3
POOL_WINDOWS = (2, 4, 8, 16)
POOL_GC = D_MODEL // len(POOL_WINDOWS)
EPS = 1e-6

F32 = jnp.float32
BF16 = jnp.bfloat16

FFN_TM = 512
MIX_TM = 512
FF_CHUNK = 256
FF_CHUNKS_BETWEEN_DOWN_HALVES = 4
CONV_HALO = 8
POOL_HALO = 16

VMEM_LIMIT_BYTES = 60 * 1024 * 1024


def _rms_norm(x, g):
    y = x * lax.rsqrt(jnp.mean(x * x, axis=-1, keepdims=True) + EPS)
    return y * g


def _dot(a, b):
    return jnp.dot(a.astype(BF16), b.astype(BF16), preferred_element_type=F32)


def _layer_resident(layer, shape):
    zeros = (0,) * len(shape)
    return pl.BlockSpec((None,) + tuple(shape), lambda i: (layer,) + zeros,
                        pipeline_mode=pl.Buffered(1))


def _ffn_kernel(x_ref, xprev_ref, pre_ref, post_ref, wgu_ref, wd_ref, o_ref, act_ref):
    i = pl.program_id(0)
    last = pl.num_programs(0) - 1
    slot = lax.rem(i, 2)
    n_chunks = D_FF // FF_CHUNK

    def pre_norm():
        return _rms_norm(x_ref[...], pre_ref[...]).astype(BF16)

    def gate_up_chunks(h, lo, hi):
        for c0 in range(lo * FF_CHUNK, hi * FF_CHUNK, FF_CHUNK):
            g = _dot(h, wgu_ref[:, c0:c0 + FF_CHUNK])
            u = _dot(h, wgu_ref[:, D_FF + c0:D_FF + c0 + FF_CHUNK])
            act_ref[slot, :, c0:c0 + FF_CHUNK] = (g * jax.nn.sigmoid(g) * u).astype(BF16)

    def residual_out(y):
        o_ref[...] = xprev_ref[...] + 0.5 * _rms_norm(y, post_ref[...])

    @pl.when(i == 0)
    def _():
        gate_up_chunks(pre_norm(), 0, n_chunks)

    @pl.when(jnp.logical_and(i > 0, i < last))
    def _():
        act = act_ref[1 - slot]
        h = pre_norm()
        y_lo = _dot(act, wd_ref[:, 0:D_MODEL // 2])
        gate_up_chunks(h, 0, FF_CHUNKS_BETWEEN_DOWN_HALVES)
        y_hi = _dot(act, wd_ref[:, D_MODEL // 2:D_MODEL])
        residual_out(jnp.concatenate([y_lo, y_hi], axis=1))
        gate_up_chunks(h, FF_CHUNKS_BETWEEN_DOWN_HALVES, n_chunks)

    @pl.when(i == last)
    def _():
        residual_out(_dot(act_ref[1 - slot], wd_ref[...]))


def _ffn(layer, x, pre, post, wgu, wd):
    n_tiles = x.shape[0] // FFN_TM
    cur_spec = pl.BlockSpec((FFN_TM, D_MODEL), lambda i: (jnp.minimum(i, n_tiles - 1), 0))
    prev_spec = pl.BlockSpec((FFN_TM, D_MODEL), lambda i: (jnp.maximum(i - 1, 0), 0))
    return pl.pallas_call(
        _ffn_kernel,
        grid=(n_tiles + 1,),
        in_specs=[
            cur_spec,
            prev_spec,
            _layer_resident(layer, (1, D_MODEL)),
            _layer_resident(layer, (1, D_MODEL)),
            _layer_resident(layer, (D_MODEL, 2 * D_FF)),
            _layer_resident(layer, (D_FF, D_MODEL)),
        ],
        out_specs=prev_spec,
        out_shape=jax.ShapeDtypeStruct(x.shape, x.dtype),
        scratch_shapes=[pltpu.VMEM((2, FFN_TM, D_FF), BF16)],
        compiler_params=pltpu.CompilerParams(
            dimension_semantics=("arbitrary",), vmem_limit_bytes=VMEM_LIMIT_BYTES),
        name="ffn",
    )(x, x, pre, post, wgu, wd)


def _mixer_kernel(x_ref, pre_ref, post_ref, win_ref, bgate_ref, convw_ref, wco_ref,
                  wpg_ref, bpg_ref, pscale_ref, wpo_ref, wo_ref, o_ref,
                  cv_ref, p_ref, yp_ref):
    tm = MIX_TM
    tile_in_seq = lax.rem(pl.program_id(0), SEQ // tm)

    @pl.when(tile_in_seq == 0)
    def _():
        cv_ref[0:CONV_HALO, :] = jnp.zeros((CONV_HALO, D_MODEL), F32)
        p_ref[0:POOL_HALO, :] = jnp.zeros((POOL_HALO, D_MODEL), F32)

    @pl.when(tile_in_seq != 0)
    def _():
        cv_ref[0:CONV_HALO, :] = cv_ref[tm:tm + CONV_HALO, :]
        p_ref[0:POOL_HALO, :] = p_ref[tm:tm + POOL_HALO, :]

    x = x_ref[...]
    h = _rms_norm(x, pre_ref[...]).astype(BF16)

    def proj(k):
        return _dot(h, win_ref[:, k * D_MODEL:(k + 1) * D_MODEL])

    p_ref[POOL_HALO:POOL_HALO + tm, :] = proj(3)
    t_seq = tile_in_seq * tm + lax.broadcasted_iota(jnp.int32, (tm, 1), 0)
    for gi, win in enumerate(POOL_WINDOWS):
        cols = slice(gi * POOL_GC, (gi + 1) * POOL_GC)
        ext = p_ref[:, cols]
        s = ext
        k = 1
        while k < win:
            s = s + pltpu.roll(s, k, axis=0)
            k *= 2
        inv_count = 1.0 / jnp.minimum(t_seq + 1, win).astype(F32)
        d = s[POOL_HALO:, :] * inv_count - ext[POOL_HALO:, :]
        yg = _dot(d, wpg_ref[gi]) + bpg_ref[:, cols]
        yp_ref[:, cols] = (yg * pscale_ref[:, cols]).astype(BF16)
    y_p = _dot(yp_ref[...], wpo_ref[...])
    g_p = jax.nn.sigmoid(proj(5) + bgate_ref[:, D_MODEL:2 * D_MODEL])
    mixed = g_p * y_p

    cv_ref[CONV_HALO:CONV_HALO + tm, :] = proj(1) * proj(2)
    cw = convw_ref[...]
    cv = cv_ref[...]
    y = (cw[0:1, :] * pltpu.roll(cv, 2, axis=0) + cw[1:2, :] * pltpu.roll(cv, 1, axis=0)
         + cw[2:3, :] * cv)
    y_a = _dot(proj(0) * y[CONV_HALO:, :], wco_ref[...])
    g_a = jax.nn.sigmoid(proj(4) + bgate_ref[:, 0:D_MODEL])
    mixed = g_a * y_a + mixed

    out = _dot(mixed, wo_ref[...])
    o_ref[...] = x + _rms_norm(out, post_ref[...])


def _mixer(layer, x, pre, post, win, bgate, convw, wco, wpg, bpg, pscale, wpo, wo):
    t = x.shape[0]
    tm = MIX_TM
    row_spec = pl.BlockSpec((tm, D_MODEL), lambda i: (i, 0))
    n_groups = len(POOL_WINDOWS)
    return pl.pallas_call(
        _mixer_kernel,
        grid=(t // tm,),
        in_specs=[
            row_spec,
            _layer_resident(layer, (1, D_MODEL)),
            _layer_resident(layer, (1, D_MODEL)),
            _layer_resident(layer, (D_MODEL, 6 * D_MODEL)),
            _layer_resident(layer, (1, 2 * D_MODEL)),
            _layer_resident(layer, (CONV_K, D_MODEL)),
            _layer_resident(layer, (D_MODEL, D_MODEL)),
            _layer_resident(layer, (n_groups, POOL_GC, POOL_GC)),
            _layer_resident(layer, (1, D_MODEL)),
            _layer_resident(layer, (1, D_MODEL)),
            _layer_resident(layer, (D_MODEL, D_MODEL)),
            _layer_resident(layer, (D_MODEL, D_MODEL)),
        ],
        out_specs=row_spec,
        out_shape=jax.ShapeDtypeStruct(x.shape, x.dtype),
        scratch_shapes=[
            pltpu.VMEM((CONV_HALO + tm, D_MODEL), F32),
            pltpu.VMEM((POOL_HALO + tm, D_MODEL), F32),
            pltpu.VMEM((tm, D_MODEL), BF16),
        ],
        compiler_params=pltpu.CompilerParams(
            dimension_semantics=("arbitrary",), vmem_limit_bytes=VMEM_LIMIT_BYTES),
        name="mixer",
    )(x, pre, post, win, bgate, convw, wco, wpg, bpg, pscale, wpo, wo)


def kernel(x, ffn1_pre, ffn1_post, ffn1_w_gate_up, ffn1_w_down, mix_pre, mix_post, w_in, b_gate,
           conv_w, w_conv_out, w_pool_group, b_pool_group, pool_scale, w_pool_out, w_o,
           ffn2_pre, ffn2_post, ffn2_w_gate_up, ffn2_w_down):
    b, s, d = x.shape
    assert (s, d) == (SEQ, D_MODEL) and (b * s) % FFN_TM == 0 and SEQ % MIX_TM == 0
    depth = w_in.shape[0]
    xf = x.reshape(b * s, d)

    def rows(v):
        return v.reshape(depth, 1, -1)

    for l in range(depth):
        xf = _ffn(l, xf, rows(ffn1_pre), rows(ffn1_post), ffn1_w_gate_up, ffn1_w_down)
        xf = _mixer(l, xf, rows(mix_pre), rows(mix_post), w_in, rows(b_gate), conv_w, w_conv_out,
                    w_pool_group, rows(b_pool_group), rows(pool_scale), w_pool_out, w_o)
        xf = _ffn(l, xf, rows(ffn2_pre), rows(ffn2_post), ffn2_w_gate_up, ffn2_w_down)
    return xf.reshape(b, s, d)
```

```python
import jax
import jax.numpy as jnp
from jax import lax
from jax.experimental import pallas as pl
from jax.experimental.pallas import tpu as pltpu

D_MODEL = 1024
SEQ = 2048
D_FF = 2816
CONV_K = 3
POOL_WINDOWS = (2, 4, 8, 16)
POOL_GC = D_MODEL // len(POOL_WINDOWS)
EPS = 1e-6

F32 = jnp.float32
BF16 = jnp.bfloat16

FFN_TM = 512
MIX_TM = 512
FF_CHUNK = 256
CONV_HALO = 8
POOL_HALO = 16

VMEM_LIMIT_BYTES = 60 * 1024 * 1024


def _rms_norm(x, g):
    y = x * lax.rsqrt(jnp.mean(x * x, axis=-1, keepdims=True) + EPS)
    return y * g


def _dot(a, b):
    return jnp.dot(a.astype(BF16), b.astype(BF16), preferred_element_type=F32)


def _layer_resident(layer, shape):
    zeros = (0,) * len(shape)
    return pl.BlockSpec((None,) + tuple(shape), lambda i: (layer,) + zeros,
                        pipeline_mode=pl.Buffered(1))


def _ffn_kernel(x_ref, pre_ref, post_ref, wgu_ref, wd_ref, o_ref, act_ref):
    x = x_ref[...]
    h = _rms_norm(x, pre_ref[...]).astype(BF16)
    for c0 in range(0, D_FF, FF_CHUNK):
        g = _dot(h, wgu_ref[:, c0:c0 + FF_CHUNK])
        u = _dot(h, wgu_ref[:, D_FF + c0:D_FF + c0 + FF_CHUNK])
        act_ref[:, c0:c0 + FF_CHUNK] = (g * jax.nn.sigmoid(g) * u).astype(BF16)
    y = _dot(act_ref[...], wd_ref[...])
    o_ref[...] = x + 0.5 * _rms_norm(y, post_ref[...])


def _ffn(layer, x, pre, post, wgu, wd):
    t = x.shape[0]
    row_spec = pl.BlockSpec((FFN_TM, D_MODEL), lambda i: (i, 0))
    return pl.pallas_call(
        _ffn_kernel,
        grid=(t // FFN_TM,),
        in_specs=[
            row_spec,
            _layer_resident(layer, (1, D_MODEL)),
            _layer_resident(layer, (1, D_MODEL)),
            _layer_resident(layer, (D_MODEL, 2 * D_FF)),
            _layer_resident(layer, (D_FF, D_MODEL)),
        ],
        out_specs=row_spec,
        out_shape=jax.ShapeDtypeStruct(x.shape, x.dtype),
        scratch_shapes=[pltpu.VMEM((FFN_TM, D_FF), BF16)],
        compiler_params=pltpu.CompilerParams(
            dimension_semantics=("arbitrary",), vmem_limit_bytes=VMEM_LIMIT_BYTES),
        name="ffn",
    )(x, pre, post, wgu, wd)


def _mixer_kernel(x_ref, pre_ref, post_ref, win_ref, bgate_ref, convw_ref, wco_ref,
                  wpg_ref, bpg_ref, pscale_ref, wpo_ref, wo_ref, o_ref,
                  cv_ref, p_ref, yp_ref):
    tm = MIX_TM
    tile_in_seq = lax.rem(pl.program_id(0), SEQ // tm)

    @pl.when(tile_in_seq == 0)
    def _():
        cv_ref[0:CONV_HALO, :] = jnp.zeros((CONV_HALO, D_MODEL), F32)
        p_ref[0:POOL_HALO, :] = jnp.zeros((POOL_HALO, D_MODEL), F32)

    @pl.when(tile_in_seq != 0)
    def _():
        cv_ref[0:CONV_HALO, :] = cv_ref[tm:tm + CONV_HALO, :]
        p_ref[0:POOL_HALO, :] = p_ref[tm:tm + POOL_HALO, :]

    x = x_ref[...]
    h = _rms_norm(x, pre_ref[...]).astype(BF16)

    def proj(k):
        return _dot(h, win_ref[:, k * D_MODEL:(k + 1) * D_MODEL])

    p_ref[POOL_HALO:POOL_HALO + tm, :] = proj(3)
    cv_ref[CONV_HALO:CONV_HALO + tm, :] = proj(1) * proj(2)

    t_seq = tile_in_seq * tm + lax.broadcasted_iota(jnp.int32, (tm, 1), 0)
    for gi, win in enumerate(POOL_WINDOWS):
        cols = slice(gi * POOL_GC, (gi + 1) * POOL_GC)
        ext = p_ref[:, cols]
        s = ext
        k = 1
        while k < win:
            s = s + pltpu.roll(s, k, axis=0)
            k *= 2
        inv_count = 1.0 / jnp.minimum(t_seq + 1, win).astype(F32)
        d = s[POOL_HALO:, :] * inv_count - ext[POOL_HALO:, :]
        yg = _dot(d, wpg_ref[gi]) + bpg_ref[:, cols]
        yp_ref[:, cols] = (yg * pscale_ref[:, cols]).astype(BF16)

    gate_b = proj(0)
    cw = convw_ref[...]
    cv = cv_ref[...]
    y = (cw[0:1, :] * pltpu.roll(cv, 2, axis=0) + cw[1:2, :] * pltpu.roll(cv, 1, axis=0)
         + cw[2:3, :] * cv)
    a = (gate_b * y[CONV_HALO:, :]).astype(BF16)
    g_a = jax.nn.sigmoid(proj(4) + bgate_ref[:, 0:D_MODEL])
    g_p = jax.nn.sigmoid(proj(5) + bgate_ref[:, D_MODEL:2 * D_MODEL])
    y_a = _dot(a, wco_ref[...])
    y_p = _dot(yp_ref[...], wpo_ref[...])
    mixed = g_a * y_a + g_p * y_p

    out = _dot(mixed, wo_ref[...])
    o_ref[...] = x + _rms_norm(out, post_ref[...])


def _mixer(layer, x, pre, post, win, bgate, convw, wco, wpg, bpg, pscale, wpo, wo):
    t = x.shape[0]
    tm = MIX_TM
    row_spec = pl.BlockSpec((tm, D_MODEL), lambda i: (i, 0))
    n_groups = len(POOL_WINDOWS)
    return pl.pallas_call(
        _mixer_kernel,
        grid=(t // tm,),
        in_specs=[
            row_spec,
            _layer_resident(layer, (1, D_MODEL)),
            _layer_resident(layer, (1, D_MODEL)),
            _layer_resident(layer, (D_MODEL, 6 * D_MODEL)),
            _layer_resident(layer, (1, 2 * D_MODEL)),
            _layer_resident(layer, (CONV_K, D_MODEL)),
            _layer_resident(layer, (D_MODEL, D_MODEL)),
            _layer_resident(layer, (n_groups, POOL_GC, POOL_GC)),
            _layer_resident(layer, (1, D_MODEL)),
            _layer_resident(layer, (1, D_MODEL)),
            _layer_resident(layer, (D_MODEL, D_MODEL)),
            _layer_resident(layer, (D_MODEL, D_MODEL)),
        ],
        out_specs=row_spec,
        out_shape=jax.ShapeDtypeStruct(x.shape, x.dtype),
        scratch_shapes=[
            pltpu.VMEM((CONV_HALO + tm, D_MODEL), F32),
            pltpu.VMEM((POOL_HALO + tm, D_MODEL), F32),
            pltpu.VMEM((tm, D_MODEL), BF16),
        ],
        compiler_params=pltpu.CompilerParams(
            dimension_semantics=("arbitrary",), vmem_limit_bytes=VMEM_LIMIT_BYTES),
        name="mixer",
    )(x, pre, post, win, bgate, convw, wco, wpg, bpg, pscale, wpo, wo)


def kernel(x, ffn1_pre, ffn1_post, ffn1_w_gate_up, ffn1_w_down, mix_pre, mix_post, w_in, b_gate,
           conv_w, w_conv_out, w_pool_group, b_pool_group, pool_scale, w_pool_out, w_o,
           ffn2_pre, ffn2_post, ffn2_w_gate_up, ffn2_w_down):
    b, s, d = x.shape
    assert (s, d) == (SEQ, D_MODEL) and (b * s) % FFN_TM == 0 and SEQ % MIX_TM == 0
    depth = w_in.shape[0]
    xf = x.reshape(b * s, d)

    def rows(v):
        return v.reshape(depth, 1, -1)

    for l in range(depth):
        xf = _ffn(l, xf, rows(ffn1_pre), rows(ffn1_post), ffn1_w_gate_up, ffn1_w_down)
        xf = _mixer(l, xf, rows(mix_pre), rows(mix_post), w_in, rows(b_gate), conv_w, w_conv_out,
                    w_pool_group, rows(b_pool_group), rows(pool_scale), w_pool_out, w_o)
        xf = _ffn(l, xf, rows(ffn2_pre), rows(ffn2_post), ffn2_w_gate_up, ffn2_w_down)
    return xf.reshape(b, s, d)
```

```python
import jax
import jax.numpy as jnp
from jax import lax
from jax.experimental import pallas as pl
from jax.experimental.pallas import tpu as pltpu

D_MODEL = 1024
SEQ = 2048
D_FF = 2816
CONV_K = 3
POOL_WINDOWS = (2, 4, 8, 16)
POOL_GC = D_MODEL // len(POOL_WINDOWS)
EPS = 1e-6

F32 = jnp.float32
BF16 = jnp.bfloat16

FFN_TM = 512
MIX_TM = 512
FF_CHUNK = 256
CONV_HALO = 8
POOL_HALO = 16

VMEM_LIMIT_BYTES = 60 * 1024 * 1024


def _rms_norm(x, g):
    y = x * lax.rsqrt(jnp.mean(x * x, axis=-1, keepdims=True) + EPS)
    return y * g


def _dot(a, b):
    return jnp.dot(a.astype(BF16), b.astype(BF16), preferred_element_type=F32)


def _layer_resident(layer, shape):
    zeros = (0,) * len(shape)
    return pl.BlockSpec((None,) + tuple(shape), lambda i: (layer,) + zeros,
                        pipeline_mode=pl.Buffered(1))


def _ffn_kernel(x_ref, pre_ref, post_ref, wgu_ref, wd_ref, o_ref, act_ref):
    x = x_ref[...]
    h = _rms_norm(x, pre_ref[...]).astype(BF16)
    for c0 in range(0, D_FF, FF_CHUNK):
        g = _dot(h, wgu_ref[:, c0:c0 + FF_CHUNK])
        u = _dot(h, wgu_ref[:, D_FF + c0:D_FF + c0 + FF_CHUNK])
        act_ref[:, c0:c0 + FF_CHUNK] = (g * jax.nn.sigmoid(g) * u).astype(BF16)
    y = _dot(act_ref[...], wd_ref[...])
    o_ref[...] = x + _rms_norm(y, 0.5 * post_ref[...])


def _ffn(layer, x, pre, post, wgu, wd):
    t = x.shape[0]
    row_spec = pl.BlockSpec((FFN_TM, D_MODEL), lambda i: (i, 0))
    return pl.pallas_call(
        _ffn_kernel,
        grid=(t // FFN_TM,),
        in_specs=[
            row_spec,
            _layer_resident(layer, (1, D_MODEL)),
            _layer_resident(layer, (1, D_MODEL)),
            _layer_resident(layer, (D_MODEL, 2 * D_FF)),
            _layer_resident(layer, (D_FF, D_MODEL)),
        ],
        out_specs=row_spec,
        out_shape=jax.ShapeDtypeStruct(x.shape, x.dtype),
        scratch_shapes=[pltpu.VMEM((FFN_TM, D_FF), BF16)],
        compiler_params=pltpu.CompilerParams(
            dimension_semantics=("arbitrary",), vmem_limit_bytes=VMEM_LIMIT_BYTES),
        name="ffn",
    )(x, pre, post, wgu, wd)


def _mixer_kernel(x_ref, pre_ref, post_ref, win_ref, bgate_ref, convw_ref, wco_ref,
                  wpg_ref, bpg_ref, pscale_ref, wpo_ref, wo_ref, o_ref,
                  cv_ref, p_ref, yp_ref):
    tm = MIX_TM
    tile_in_seq = lax.rem(pl.program_id(0), SEQ // tm)

    @pl.when(tile_in_seq == 0)
    def _():
        cv_ref[0:CONV_HALO, :] = jnp.zeros((CONV_HALO, D_MODEL), F32)
        p_ref[0:POOL_HALO, :] = jnp.zeros((POOL_HALO, D_MODEL), F32)

    @pl.when(tile_in_seq != 0)
    def _():
        cv_ref[0:CONV_HALO, :] = cv_ref[tm:tm + CONV_HALO, :]
        p_ref[0:POOL_HALO, :] = p_ref[tm:tm + POOL_HALO, :]

    x = x_ref[...]
    h = _rms_norm(x, pre_ref[...]).astype(BF16)

    def proj(k):
        return _dot(h, win_ref[:, k * D_MODEL:(k + 1) * D_MODEL])

    p_ref[POOL_HALO:POOL_HALO + tm, :] = proj(3)
    cv_ref[CONV_HALO:CONV_HALO + tm, :] = proj(1) * proj(2)

    t_seq = tile_in_seq * tm + lax.broadcasted_iota(jnp.int32, (tm, 1), 0)
    for gi, win in enumerate(POOL_WINDOWS):
        cols = slice(gi * POOL_GC, (gi + 1) * POOL_GC)
        ext = p_ref[:, cols]
        s = ext
        k = 1
        while k < win:
            s = s + pltpu.roll(s, k, axis=0)
            k *= 2
        inv_count = 1.0 / jnp.minimum(t_seq + 1, win).astype(F32)
        d = s[POOL_HALO:, :] * inv_count - ext[POOL_HALO:, :]
        yg = _dot(d, wpg_ref[gi]) + bpg_ref[:, cols]
        yp_ref[:, cols] = (yg * pscale_ref[:, cols]).astype(BF16)

    gate_b = proj(0)
    cw = convw_ref[...]
    cv = cv_ref[...]
    y = (cw[0:1, :] * pltpu.roll(cv, 2, axis=0) + cw[1:2, :] * pltpu.roll(cv, 1, axis=0)
         + cw[2:3, :] * cv)
    a = (gate_b * y[CONV_HALO:, :]).astype(BF16)
    g_a = jax.nn.sigmoid(proj(4) + bgate_ref[:, 0:D_MODEL])
    g_p = jax.nn.sigmoid(proj(5) + bgate_ref[:, D_MODEL:2 * D_MODEL])
    y_a = _dot(a, wco_ref[...])
    y_p = _dot(yp_ref[...], wpo_ref[...])
    mixed = g_a * y_a + g_p * y_p

    out = _dot(mixed, wo_ref[...])
    o_ref[...] = x + _rms_norm(out, post_ref[...])


def _mixer(layer, x, pre, post, win, bgate, convw, wco, wpg, bpg, pscale, wpo, wo):
    t = x.shape[0]
    tm = MIX_TM
    row_spec = pl.BlockSpec((tm, D_MODEL), lambda i: (i, 0))
    n_groups = len(POOL_WINDOWS)
    return pl.pallas_call(
        _mixer_kernel,
        grid=(t // tm,),
        in_specs=[
            row_spec,
            _layer_resident(layer, (1, D_MODEL)),
            _layer_resident(layer, (1, D_MODEL)),
            _layer_resident(layer, (D_MODEL, 6 * D_MODEL)),
            _layer_resident(layer, (1, 2 * D_MODEL)),
            _layer_resident(layer, (CONV_K, D_MODEL)),
            _layer_resident(layer, (D_MODEL, D_MODEL)),
            _layer_resident(layer, (n_groups, POOL_GC, POOL_GC)),
            _layer_resident(layer, (1, D_MODEL)),
            _layer_resident(layer, (1, D_MODEL)),
            _layer_resident(layer, (D_MODEL, D_MODEL)),
            _layer_resident(layer, (D_MODEL, D_MODEL)),
        ],
        out_specs=row_spec,
        out_shape=jax.ShapeDtypeStruct(x.shape, x.dtype),
        scratch_shapes=[
            pltpu.VMEM((CONV_HALO + tm, D_MODEL), F32),
            pltpu.VMEM((POOL_HALO + tm, D_MODEL), F32),
            pltpu.VMEM((tm, D_MODEL), BF16),
        ],
        compiler_params=pltpu.CompilerParams(
            dimension_semantics=("arbitrary",), vmem_limit_bytes=VMEM_LIMIT_BYTES),
        name="mixer",
    )(x, pre, post, win, bgate, convw, wco, wpg, bpg, pscale, wpo, wo)


def kernel(x, ffn1_pre, ffn1_post, ffn1_w_gate_up, ffn1_w_down, mix_pre, mix_post, w_in, b_gate,
           conv_w, w_conv_out, w_pool_group, b_pool_group, pool_scale, w_pool_out, w_o,
           ffn2_pre, ffn2_post, ffn2_w_gate_up, ffn2_w_down):
    b, s, d = x.shape
    assert (s, d) == (SEQ, D_MODEL) and (b * s) % FFN_TM == 0 and SEQ % MIX_TM == 0
    depth = w_in.shape[0]
    xf = x.reshape(b * s, d)

    def rows(v):
        return v.reshape(depth, 1, -1)

    for l in range(depth):
        xf = _ffn(l, xf, rows(ffn1_pre), rows(ffn1_post), ffn1_w_gate_up, ffn1_w_down)
        xf = _mixer(l, xf, rows(mix_pre), rows(mix_post), w_in, rows(b_gate), conv_w, w_conv_out,
                    w_pool_group, rows(b_pool_group), rows(pool_scale), w_pool_out, w_o)
        xf = _ffn(l, xf, rows(ffn2_pre), rows(ffn2_post), ffn2_w_gate_up, ffn2_w_down)
    return xf.reshape(b, s, d)
```

```python
import jax
import jax.numpy as jnp
from jax import lax
from jax.experimental import pallas as pl
from jax.experimental.pallas import tpu as pltpu

D_MODEL = 1024
SEQ = 2048
D_FF = 2816
CONV_K = 3
POOL_WINDOWS = (2, 4, 8, 16)
POOL_GC = D_MODEL // len(POOL_WINDOWS)
EPS = 1e-6

F32 = jnp.float32
BF16 = jnp.bfloat16

FFN_TM = 512
MIX_TM = 512
FF_CHUNK = 256
CONV_HALO = 8
POOL_HALO = 16

VMEM_LIMIT_BYTES = 60 * 1024 * 1024


def _inv_rms(x):
    return lax.rsqrt(jnp.mean(x * x, axis=-1, keepdims=True) + EPS)


def _rms_norm(x, g):
    return x * _inv_rms(x) * g


def _dot(a, b):
    return jnp.dot(a.astype(BF16), b.astype(BF16), preferred_element_type=F32)


def _pre_norm_split(x, g):
    return (x * g).astype(BF16), _inv_rms(x)


def _layer_resident(layer, shape):
    zeros = (0,) * len(shape)
    return pl.BlockSpec((None,) + tuple(shape), lambda i: (layer,) + zeros,
                        pipeline_mode=pl.Buffered(1))


def _ffn_kernel(x_ref, pre_ref, post_ref, wgu_ref, wd_ref, o_ref, act_ref):
    x = x_ref[...]
    h, r = _pre_norm_split(x, pre_ref[...])
    for c0 in range(0, D_FF, FF_CHUNK):
        g = r * _dot(h, wgu_ref[:, c0:c0 + FF_CHUNK])
        u = r * _dot(h, wgu_ref[:, D_FF + c0:D_FF + c0 + FF_CHUNK])
        act_ref[:, c0:c0 + FF_CHUNK] = (g * jax.nn.sigmoid(g) * u).astype(BF16)
    y = _dot(act_ref[...], wd_ref[...])
    o_ref[...] = x + _rms_norm(y, 0.5 * post_ref[...])


def _ffn(layer, x, pre, post, wgu, wd):
    t = x.shape[0]
    row_spec = pl.BlockSpec((FFN_TM, D_MODEL), lambda i: (i, 0))
    return pl.pallas_call(
        _ffn_kernel,
        grid=(t // FFN_TM,),
        in_specs=[
            row_spec,
            _layer_resident(layer, (1, D_MODEL)),
            _layer_resident(layer, (1, D_MODEL)),
            _layer_resident(layer, (D_MODEL, 2 * D_FF)),
            _layer_resident(layer, (D_FF, D_MODEL)),
        ],
        out_specs=row_spec,
        out_shape=jax.ShapeDtypeStruct(x.shape, x.dtype),
        scratch_shapes=[pltpu.VMEM((FFN_TM, D_FF), BF16)],
        compiler_params=pltpu.CompilerParams(
            dimension_semantics=("arbitrary",), vmem_limit_bytes=VMEM_LIMIT_BYTES),
        name="ffn",
    )(x, pre, post, wgu, wd)


def _mixer_kernel(x_ref, pre_ref, post_ref, win_ref, bgate_ref, convw_ref, wco_ref,
                  wpg_ref, bpg_ref, pscale_ref, wpo_ref, wo_ref, o_ref,
                  cv_ref, p_ref, yp_ref):
    tm = MIX_TM
    tile_in_seq = lax.rem(pl.program_id(0), SEQ // tm)

    @pl.when(tile_in_seq == 0)
    def _():
        cv_ref[0:CONV_HALO, :] = jnp.zeros((CONV_HALO, D_MODEL), F32)
        p_ref[0:POOL_HALO, :] = jnp.zeros((POOL_HALO, D_MODEL), F32)

    @pl.when(tile_in_seq != 0)
    def _():
        cv_ref[0:CONV_HALO, :] = cv_ref[tm:tm + CONV_HALO, :]
        p_ref[0:POOL_HALO, :] = p_ref[tm:tm + POOL_HALO, :]

    x = x_ref[...]
    h, r = _pre_norm_split(x, pre_ref[...])

    def proj(k):
        return r * _dot(h, win_ref[:, k * D_MODEL:(k + 1) * D_MODEL])

    p_ref[POOL_HALO:POOL_HALO + tm, :] = proj(3)
    cv_ref[CONV_HALO:CONV_HALO + tm, :] = proj(1) * proj(2)

    t_seq = tile_in_seq * tm + lax.broadcasted_iota(jnp.int32, (tm, 1), 0)
    for gi, win in enumerate(POOL_WINDOWS):
        cols = slice(gi * POOL_GC, (gi + 1) * POOL_GC)
        ext = p_ref[:, cols]
        s = ext
        k = 1
        while k < win:
            s = s + pltpu.roll(s, k, axis=0)
            k *= 2
        inv_count = 1.0 / jnp.minimum(t_seq + 1, win).astype(F32)
        d = s[POOL_HALO:, :] * inv_count - ext[POOL_HALO:, :]
        yg = _dot(d, wpg_ref[gi]) + bpg_ref[:, cols]
        yp_ref[:, cols] = (yg * pscale_ref[:, cols]).astype(BF16)

    gate_b = proj(0)
    cw = convw_ref[...]
    cv = cv_ref[...]
    y = (cw[0:1, :] * pltpu.roll(cv, 2, axis=0) + cw[1:2, :] * pltpu.roll(cv, 1, axis=0)
         + cw[2:3, :] * cv)
    a = (gate_b * y[CONV_HALO:, :]).astype(BF16)
    g_a = jax.nn.sigmoid(proj(4) + bgate_ref[:, 0:D_MODEL])
    g_p = jax.nn.sigmoid(proj(5) + bgate_ref[:, D_MODEL:2 * D_MODEL])
    y_a = _dot(a, wco_ref[...])
    y_p = _dot(yp_ref[...], wpo_ref[...])
    mixed = g_a * y_a + g_p * y_p

    out = _dot(mixed, wo_ref[...])
    o_ref[...] = x + _rms_norm(out, post_ref[...])


def _mixer(layer, x, pre, post, win, bgate, convw, wco, wpg, bpg, pscale, wpo, wo):
    t = x.shape[0]
    tm = MIX_TM
    row_spec = pl.BlockSpec((tm, D_MODEL), lambda i: (i, 0))
    n_groups = len(POOL_WINDOWS)
    return pl.pallas_call(
        _mixer_kernel,
        grid=(t // tm,),
        in_specs=[
            row_spec,
            _layer_resident(layer, (1, D_MODEL)),
            _layer_resident(layer, (1, D_MODEL)),
            _layer_resident(layer, (D_MODEL, 6 * D_MODEL)),
            _layer_resident(layer, (1, 2 * D_MODEL)),
            _layer_resident(layer, (CONV_K, D_MODEL)),
            _layer_resident(layer, (D_MODEL, D_MODEL)),
            _layer_resident(layer, (n_groups, POOL_GC, POOL_GC)),
            _layer_resident(layer, (1, D_MODEL)),
            _layer_resident(layer, (1, D_MODEL)),
            _layer_resident(layer, (D_MODEL, D_MODEL)),
            _layer_resident(layer, (D_MODEL, D_MODEL)),
        ],
        out_specs=row_spec,
        out_shape=jax.ShapeDtypeStruct(x.shape, x.dtype),
        scratch_shapes=[
            pltpu.VMEM((CONV_HALO + tm, D_MODEL), F32),
            pltpu.VMEM((POOL_HALO + tm, D_MODEL), F32),
            pltpu.VMEM((tm, D_MODEL), BF16),
        ],
        compiler_params=pltpu.CompilerParams(
            dimension_semantics=("arbitrary",), vmem_limit_bytes=VMEM_LIMIT_BYTES),
        name="mixer",
    )(x, pre, post, win, bgate, convw, wco, wpg, bpg, pscale, wpo, wo)


def kernel(x, ffn1_pre, ffn1_post, ffn1_w_gate_up, ffn1_w_down, mix_pre, mix_post, w_in, b_gate,
           conv_w, w_conv_out, w_pool_group, b_pool_group, pool_scale, w_pool_out, w_o,
           ffn2_pre, ffn2_post, ffn2_w_gate_up, ffn2_w_down):
    b, s, d = x.shape
    assert (s, d) == (SEQ, D_MODEL) and (b * s) % FFN_TM == 0 and SEQ % MIX_TM == 0
    depth = w_in.shape[0]
    xf = x.reshape(b * s, d)

    def rows(v):
        return v.reshape(depth, 1, -1)

    for l in range(depth):
        xf = _ffn(l, xf, rows(ffn1_pre), rows(ffn1_post), ffn1_w_gate_up, ffn1_w_down)
        xf = _mixer(l, xf, rows(mix_pre), rows(mix_post), w_in, rows(b_gate), conv_w, w_conv_out,
                    w_pool_group, rows(b_pool_group), rows(pool_scale), w_pool_out, w_o)
        xf = _ffn(l, xf, rows(ffn2_pre), rows(ffn2_post), ffn2_w_gate_up, ffn2_w_down)
    return xf.reshape(b, s, d)
```

```python
import jax
import jax.numpy as jnp
from jax import lax
from jax.experimental import pallas as pl
from jax.experimental.pallas import tpu as pltpu

D_MODEL = 1024
SEQ = 2048
D_FF = 2816
CONV_K = 3
POOL_WINDOWS = (2, 4, 8, 16)
POOL_GC = D_MODEL // len(POOL_WINDOWS)
EPS = 1e-6

F32 = jnp.float32
BF16 = jnp.bfloat16

FFN_TM = 512
FFN_SUBTILES = 2
MIX_TM = 512
FF_CHUNK = 256
CONV_HALO = 8
POOL_HALO = 16

VMEM_LIMIT_BYTES = 60 * 1024 * 1024


def _inv_rms(x):
    return lax.rsqrt(jnp.mean(x * x, axis=-1, keepdims=True) + EPS)


def _rms_norm(x, g):
    return x * _inv_rms(x) * g


def _dot(a, b):
    return jnp.dot(a.astype(BF16), b.astype(BF16), preferred_element_type=F32)


def _pre_norm_split(x, g):
    return (x * g).astype(BF16), _inv_rms(x)


def _layer_resident(layer, shape):
    zeros = (0,) * len(shape)
    return pl.BlockSpec((None,) + tuple(shape), lambda i: (layer,) + zeros,
                        pipeline_mode=pl.Buffered(1))


def _ffn_kernel(x_ref, pre_ref, post_ref, wgu_ref, wd_ref, o_ref, act_ref):
    half_gain = 0.5 * post_ref[...]
    for s in range(FFN_SUBTILES):
        rows = slice(s * FFN_TM, (s + 1) * FFN_TM)
        x = x_ref[rows, :]
        h, r = _pre_norm_split(x, pre_ref[...])
        for c0 in range(0, D_FF, FF_CHUNK):
            g = r * _dot(h, wgu_ref[:, c0:c0 + FF_CHUNK])
            u = r * _dot(h, wgu_ref[:, D_FF + c0:D_FF + c0 + FF_CHUNK])
            act_ref[s, :, c0:c0 + FF_CHUNK] = (g * jax.nn.sigmoid(g) * u).astype(BF16)
        y = _dot(act_ref[s], wd_ref[...])
        o_ref[rows, :] = x + _rms_norm(y, half_gain)


def _ffn(layer, x, pre, post, wgu, wd):
    t = x.shape[0]
    block_rows = FFN_SUBTILES * FFN_TM
    row_spec = pl.BlockSpec((block_rows, D_MODEL), lambda i: (i, 0))
    return pl.pallas_call(
        _ffn_kernel,
        grid=(t // block_rows,),
        in_specs=[
            row_spec,
            _layer_resident(layer, (1, D_MODEL)),
            _layer_resident(layer, (1, D_MODEL)),
            _layer_resident(layer, (D_MODEL, 2 * D_FF)),
            _layer_resident(layer, (D_FF, D_MODEL)),
        ],
        out_specs=row_spec,
        out_shape=jax.ShapeDtypeStruct(x.shape, x.dtype),
        scratch_shapes=[pltpu.VMEM((FFN_SUBTILES, FFN_TM, D_FF), BF16)],
        compiler_params=pltpu.CompilerParams(
            dimension_semantics=("arbitrary",), vmem_limit_bytes=VMEM_LIMIT_BYTES),
        name="ffn",
    )(x, pre, post, wgu, wd)


def _mixer_kernel(x_ref, pre_ref, post_ref, win_ref, bgate_ref, convw_ref, wco_ref,
                  wpg_ref, bpg_ref, pscale_ref, wpo_ref, wo_ref, o_ref,
                  cv_ref, p_ref, yp_ref):
    tm = MIX_TM
    tile_in_seq = lax.rem(pl.program_id(0), SEQ // tm)

    @pl.when(tile_in_seq == 0)
    def _():
        cv_ref[0:CONV_HALO, :] = jnp.zeros((CONV_HALO, D_MODEL), F32)
        p_ref[0:POOL_HALO, :] = jnp.zeros((POOL_HALO, D_MODEL), F32)

    @pl.when(tile_in_seq != 0)
    def _():
        cv_ref[0:CONV_HALO, :] = cv_ref[tm:tm + CONV_HALO, :]
        p_ref[0:POOL_HALO, :] = p_ref[tm:tm + POOL_HALO, :]

    x = x_ref[...]
    h, r = _pre_norm_split(x, pre_ref[...])

    def proj(k):
        return r * _dot(h, win_ref[:, k * D_MODEL:(k + 1) * D_MODEL])

    p_ref[POOL_HALO:POOL_HALO + tm, :] = proj(3)
    cv_ref[CONV_HALO:CONV_HALO + tm, :] = proj(1) * proj(2)

    t_seq = tile_in_seq * tm + lax.broadcasted_iota(jnp.int32, (tm, 1), 0)
    for gi, win in enumerate(POOL_WINDOWS):
        cols = slice(gi * POOL_GC, (gi + 1) * POOL_GC)
        ext = p_ref[:, cols]
        s = ext
        k = 1
        while k < win:
            s = s + pltpu.roll(s, k, axis=0)
            k *= 2
        inv_count = 1.0 / jnp.minimum(t_seq + 1, win).astype(F32)
        d = s[POOL_HALO:, :] * inv_count - ext[POOL_HALO:, :]
        yg = _dot(d, wpg_ref[gi]) + bpg_ref[:, cols]
        yp_ref[:, cols] = (yg * pscale_ref[:, cols]).astype(BF16)

    gate_b = proj(0)
    cw = convw_ref[...]
    cv = cv_ref[...]
    y = (cw[0:1, :] * pltpu.roll(cv, 2, axis=0) + cw[1:2, :] * pltpu.roll(cv, 1, axis=0)
         + cw[2:3, :] * cv)
    a = (gate_b * y[CONV_HALO:, :]).astype(BF16)
    g_a = jax.nn.sigmoid(proj(4) + bgate_ref[:, 0:D_MODEL])
    g_p = jax.nn.sigmoid(proj(5) + bgate_ref[:, D_MODEL:2 * D_MODEL])
    y_a = _dot(a, wco_ref[...])
    y_p = _dot(yp_ref[...], wpo_ref[...])
    mixed = g_a * y_a + g_p * y_p

    out = _dot(mixed, wo_ref[...])
    o_ref[...] = x + _rms_norm(out, post_ref[...])


def _mixer(layer, x, pre, post, win, bgate, convw, wco, wpg, bpg, pscale, wpo, wo):
    t = x.shape[0]
    tm = MIX_TM
    row_spec = pl.BlockSpec((tm, D_MODEL), lambda i: (i, 0))
    n_groups = len(POOL_WINDOWS)
    return pl.pallas_call(
        _mixer_kernel,
        grid=(t // tm,),
        in_specs=[
            row_spec,
            _layer_resident(layer, (1, D_MODEL)),
            _layer_resident(layer, (1, D_MODEL)),
            _layer_resident(layer, (D_MODEL, 6 * D_MODEL)),
            _layer_resident(layer, (1, 2 * D_MODEL)),
            _layer_resident(layer, (CONV_K, D_MODEL)),
            _layer_resident(layer, (D_MODEL, D_MODEL)),
            _layer_resident(layer, (n_groups, POOL_GC, POOL_GC)),
            _layer_resident(layer, (1, D_MODEL)),
            _layer_resident(layer, (1, D_MODEL)),
            _layer_resident(layer, (D_MODEL, D_MODEL)),
            _layer_resident(layer, (D_MODEL, D_MODEL)),
        ],
        out_specs=row_spec,
        out_shape=jax.ShapeDtypeStruct(x.shape, x.dtype),
        scratch_shapes=[
            pltpu.VMEM((CONV_HALO + tm, D_MODEL), F32),
            pltpu.VMEM((POOL_HALO + tm, D_MODEL), F32),
            pltpu.VMEM((tm, D_MODEL), BF16),
        ],
        compiler_params=pltpu.CompilerParams(
            dimension_semantics=("arbitrary",), vmem_limit_bytes=VMEM_LIMIT_BYTES),
        name="mixer",
    )(x, pre, post, win, bgate, convw, wco, wpg, bpg, pscale, wpo, wo)


def kernel(x, ffn1_pre, ffn1_post, ffn1_w_gate_up, ffn1_w_down, mix_pre, mix_post, w_in, b_gate,
           conv_w, w_conv_out, w_pool_group, b_pool_group, pool_scale, w_pool_out, w_o,
           ffn2_pre, ffn2_post, ffn2_w_gate_up, ffn2_w_down):
    b, s, d = x.shape
    assert (s, d) == (SEQ, D_MODEL) and (b * s) % (FFN_SUBTILES * FFN_TM) == 0 and SEQ % MIX_TM == 0
    depth = w_in.shape[0]
    xf = x.reshape(b * s, d)

    def rows(v):
        return v.reshape(depth, 1, -1)

    for l in range(depth):
        xf = _ffn(l, xf, rows(ffn1_pre), rows(ffn1_post), ffn1_w_gate_up, ffn1_w_down)
        xf = _mixer(l, xf, rows(mix_pre), rows(mix_post), w_in, rows(b_gate), conv_w, w_conv_out,
                    w_pool_group, rows(b_pool_group), rows(pool_scale), w_pool_out, w_o)
        xf = _ffn(l, xf, rows(ffn2_pre), rows(ffn2_post), ffn2_w_gate_up, ffn2_w_down)
    return xf.reshape(b, s, d)
```

```python
import functools

import jax
import jax.numpy as jnp
from jax import lax
from jax.experimental import pallas as pl
from jax.experimental.pallas import tpu as pltpu

D_MODEL = 1024
SEQ = 2048
D_FF = 2816
CONV_K = 3
POOL_WINDOWS = (2, 4, 8, 16)
POOL_GC = D_MODEL // len(POOL_WINDOWS)
EPS = 1e-6

F32 = jnp.float32
BF16 = jnp.bfloat16

FFN_TM = 512
MIX_TM = 512
FF_CHUNK = 256
CONV_HALO = 8
POOL_HALO = 16

VMEM_LIMIT_BYTES = 60 * 1024 * 1024

_IN_HBM = pl.BlockSpec(memory_space=pl.ANY)


def _inv_rms(x):
    return lax.rsqrt(jnp.mean(x * x, axis=-1, keepdims=True) + EPS)


def _rms_norm(x, g):
    return x * _inv_rms(x) * g


def _dot(a, b):
    return jnp.dot(a.astype(BF16), b.astype(BF16), preferred_element_type=F32)


def _pre_norm_split(x, g):
    return (x * g).astype(BF16), _inv_rms(x)


def _layer_resident(layer, shape):
    zeros = (0,) * len(shape)
    return pl.BlockSpec((None,) + tuple(shape), lambda i: (layer,) + zeros,
                        pipeline_mode=pl.Buffered(1))


def _first_step_waits(tile):
    @pl.when(pl.program_id(0) == 0)
    def _():
        tile(True)

    @pl.when(pl.program_id(0) != 0)
    def _():
        tile(False)


def _ffn_kernel(layer, x_ref, pre_ref, post_ref, wgu_hbm, wd_hbm, o_ref,
                act_ref, wgu_ref, wd_ref, sems):
    n_chunks = D_FF // FF_CHUNK

    def gate_up_copy(which, c):
        cols = pl.ds(which * D_FF + c * FF_CHUNK, FF_CHUNK)
        return pltpu.make_async_copy(wgu_hbm.at[layer, :, cols], wgu_ref.at[:, cols],
                                     sems.at[which * n_chunks + c])

    def down_copy():
        return pltpu.make_async_copy(wd_hbm.at[layer], wd_ref, sems.at[2 * n_chunks])

    def tile(wait_for_weights):
        if wait_for_weights:
            for c in range(n_chunks):
                gate_up_copy(0, c).start()
                gate_up_copy(1, c).start()
            down_copy().start()
        x = x_ref[...]
        h, r = _pre_norm_split(x, pre_ref[...])
        for c in range(n_chunks):
            c0 = c * FF_CHUNK
            if wait_for_weights:
                gate_up_copy(0, c).wait()
                gate_up_copy(1, c).wait()
            g = r * _dot(h, wgu_ref[:, c0:c0 + FF_CHUNK])
            u = r * _dot(h, wgu_ref[:, D_FF + c0:D_FF + c0 + FF_CHUNK])
            act_ref[:, c0:c0 + FF_CHUNK] = (g * jax.nn.sigmoid(g) * u).astype(BF16)
        if wait_for_weights:
            down_copy().wait()
        y = _dot(act_ref[...], wd_ref[...])
        o_ref[...] = x + _rms_norm(y, 0.5 * post_ref[...])

    _first_step_waits(tile)


def _ffn(layer, x, pre, post, wgu, wd):
    t = x.shape[0]
    row_spec = pl.BlockSpec((FFN_TM, D_MODEL), lambda i: (i, 0))
    n_copies = 2 * (D_FF // FF_CHUNK) + 1
    return pl.pallas_call(
        functools.partial(_ffn_kernel, layer),
        grid=(t // FFN_TM,),
        in_specs=[
            row_spec,
            _layer_resident(layer, (1, D_MODEL)),
            _layer_resident(layer, (1, D_MODEL)),
            _IN_HBM,
            _IN_HBM,
        ],
        out_specs=row_spec,
        out_shape=jax.ShapeDtypeStruct(x.shape, x.dtype),
        scratch_shapes=[
            pltpu.VMEM((FFN_TM, D_FF), BF16),
            pltpu.VMEM((D_MODEL, 2 * D_FF), F32),
            pltpu.VMEM((D_FF, D_MODEL), F32),
            pltpu.SemaphoreType.DMA((n_copies,)),
        ],
        compiler_params=pltpu.CompilerParams(
            dimension_semantics=("arbitrary",), vmem_limit_bytes=VMEM_LIMIT_BYTES),
        name="ffn",
    )(x, pre, post, wgu, wd)


_MIX_WIN_ORDER = (3, 1, 2, 0, 4, 5)
_MIX_WIN_BEFORE_POOL_GROUPS = 3


def _mixer_kernel(layer, x_ref, pre_ref, post_ref, win_hbm, bgate_ref, convw_ref, wco_hbm,
                  wpg_hbm, bpg_ref, pscale_ref, wpo_hbm, wo_hbm, o_ref,
                  cv_ref, p_ref, yp_ref, win_ref, wco_ref, wpg_ref, wpo_ref, wo_ref, sems):
    tm = MIX_TM
    tile_in_seq = lax.rem(pl.program_id(0), SEQ // tm)
    n_win = len(_MIX_WIN_ORDER)

    def win_copy(k):
        cols = pl.ds(k * D_MODEL, D_MODEL)
        return pltpu.make_async_copy(win_hbm.at[layer, :, cols], win_ref.at[:, cols], sems.at[k])

    def wpg_copy():
        return pltpu.make_async_copy(wpg_hbm.at[layer], wpg_ref, sems.at[n_win])

    def wco_copy():
        return pltpu.make_async_copy(wco_hbm.at[layer], wco_ref, sems.at[n_win + 1])

    def wpo_copy():
        return pltpu.make_async_copy(wpo_hbm.at[layer], wpo_ref, sems.at[n_win + 2])

    def wo_copy():
        return pltpu.make_async_copy(wo_hbm.at[layer], wo_ref, sems.at[n_win + 3])

    @pl.when(tile_in_seq == 0)
    def _():
        cv_ref[0:CONV_HALO, :] = jnp.zeros((CONV_HALO, D_MODEL), F32)
        p_ref[0:POOL_HALO, :] = jnp.zeros((POOL_HALO, D_MODEL), F32)

    @pl.when(tile_in_seq != 0)
    def _():
        cv_ref[0:CONV_HALO, :] = cv_ref[tm:tm + CONV_HALO, :]
        p_ref[0:POOL_HALO, :] = p_ref[tm:tm + POOL_HALO, :]

    def tile(wait_for_weights):
        def arrived(copy):
            if wait_for_weights:
                copy.wait()

        if wait_for_weights:
            for k in _MIX_WIN_ORDER[:_MIX_WIN_BEFORE_POOL_GROUPS]:
                win_copy(k).start()
            wpg_copy().start()
            for k in _MIX_WIN_ORDER[_MIX_WIN_BEFORE_POOL_GROUPS:]:
                win_copy(k).start()
            wco_copy().start()
            wpo_copy().start()
            wo_copy().start()

        x = x_ref[...]
        h, r = _pre_norm_split(x, pre_ref[...])

        def proj(k):
            arrived(win_copy(k))
            return r * _dot(h, win_ref[:, k * D_MODEL:(k + 1) * D_MODEL])

        p_ref[POOL_HALO:POOL_HALO + tm, :] = proj(3)
        cv_ref[CONV_HALO:CONV_HALO + tm, :] = proj(1) * proj(2)

        arrived(wpg_copy())
        t_seq = tile_in_seq * tm + lax.broadcasted_iota(jnp.int32, (tm, 1), 0)
        for gi, win in enumerate(POOL_WINDOWS):
            cols = slice(gi * POOL_GC, (gi + 1) * POOL_GC)
            ext = p_ref[:, cols]
            s = ext
            k = 1
            while k < win:
                s = s + pltpu.roll(s, k, axis=0)
                k *= 2
            inv_count = 1.0 / jnp.minimum(t_seq + 1, win).astype(F32)
            d = s[POOL_HALO:, :] * inv_count - ext[POOL_HALO:, :]
            yg = _dot(d, wpg_ref[gi]) + bpg_ref[:, cols]
            yp_ref[:, cols] = (yg * pscale_ref[:, cols]).astype(BF16)

        gate_b = proj(0)
        cw = convw_ref[...]
        cv = cv_ref[...]
        y = (cw[0:1, :] * pltpu.roll(cv, 2, axis=0) + cw[1:2, :] * pltpu.roll(cv, 1, axis=0)
             + cw[2:3, :] * cv)
        a = (gate_b * y[CONV_HALO:, :]).astype(BF16)
        g_a = jax.nn.sigmoid(proj(4) + bgate_ref[:, 0:D_MODEL])
        g_p = jax.nn.sigmoid(proj(5) + bgate_ref[:, D_MODEL:2 * D_MODEL])
        arrived(wco_copy())
        y_a = _dot(a, wco_ref[...])
        arrived(wpo_copy())
        y_p = _dot(yp_ref[...], wpo_ref[...])
        mixed = g_a * y_a + g_p * y_p

        arrived(wo_copy())
        out = _dot(mixed, wo_ref[...])
        o_ref[...] = x + _rms_norm(out, post_ref[...])

    _first_step_waits(tile)


def _mixer(layer, x, pre, post, win, bgate, convw, wco, wpg, bpg, pscale, wpo, wo):
    t = x.shape[0]
    tm = MIX_TM
    row_spec = pl.BlockSpec((tm, D_MODEL), lambda i: (i, 0))
    n_groups = len(POOL_WINDOWS)
    return pl.pallas_call(
        functools.partial(_mixer_kernel, layer),
        grid=(t // tm,),
        in_specs=[
            row_spec,
            _layer_resident(layer, (1, D_MODEL)),
            _layer_resident(layer, (1, D_MODEL)),
            _IN_HBM,
            _layer_resident(layer, (1, 2 * D_MODEL)),
            _layer_resident(layer, (CONV_K, D_MODEL)),
            _IN_HBM,
            _IN_HBM,
            _layer_resident(layer, (1, D_MODEL)),
            _layer_resident(layer, (1, D_MODEL)),
            _IN_HBM,
            _IN_HBM,
        ],
        out_specs=row_spec,
        out_shape=jax.ShapeDtypeStruct(x.shape, x.dtype),
        scratch_shapes=[
            pltpu.VMEM((CONV_HALO + tm, D_MODEL), F32),
            pltpu.VMEM((POOL_HALO + tm, D_MODEL), F32),
            pltpu.VMEM((tm, D_MODEL), BF16),
            pltpu.VMEM((D_MODEL, 6 * D_MODEL), F32),
            pltpu.VMEM((D_MODEL, D_MODEL), F32),
            pltpu.VMEM((n_groups, POOL_GC, POOL_GC), F32),
            pltpu.VMEM((D_MODEL, D_MODEL), F32),
            pltpu.VMEM((D_MODEL, D_MODEL), F32),
            pltpu.SemaphoreType.DMA((len(_MIX_WIN_ORDER) + 4,)),
        ],
        compiler_params=pltpu.CompilerParams(
            dimension_semantics=("arbitrary",), vmem_limit_bytes=VMEM_LIMIT_BYTES),
        name="mixer",
    )(x, pre, post, win, bgate, convw, wco, wpg, bpg, pscale, wpo, wo)


def kernel(x, ffn1_pre, ffn1_post, ffn1_w_gate_up, ffn1_w_down, mix_pre, mix_post, w_in, b_gate,
           conv_w, w_conv_out, w_pool_group, b_pool_group, pool_scale, w_pool_out, w_o,
           ffn2_pre, ffn2_post, ffn2_w_gate_up, ffn2_w_down):
    b, s, d = x.shape
    assert (s, d) == (SEQ, D_MODEL) and (b * s) % FFN_TM == 0 and SEQ % MIX_TM == 0
    depth = w_in.shape[0]
    xf = x.reshape(b * s, d)

    def rows(v):
        return v.reshape(depth, 1, -1)

    for l in range(depth):
        xf = _ffn(l, xf, rows(ffn1_pre), rows(ffn1_post), ffn1_w_gate_up, ffn1_w_down)
        xf = _mixer(l, xf, rows(mix_pre), rows(mix_post), w_in, rows(b_gate), conv_w, w_conv_out,
                    w_pool_group, rows(b_pool_group), rows(pool_scale), w_pool_out, w_o)
        xf = _ffn(l, xf, rows(ffn2_pre), rows(ffn2_post), ffn2_w_gate_up, ffn2_w_down)
    return xf.reshape(b, s, d)
```

```python
import functools

import jax
import jax.numpy as jnp
from jax import lax
from jax.experimental import pallas as pl
from jax.experimental.pallas import tpu as pltpu

D_MODEL = 1024
SEQ = 2048
D_FF = 2816
CONV_K = 3
POOL_WINDOWS = (2, 4, 8, 16)
POOL_GC = D_MODEL // len(POOL_WINDOWS)
EPS = 1e-6

F32 = jnp.float32
BF16 = jnp.bfloat16

FFN_TM = 512
MIX_TM = 512
FF_CHUNK = 256
CONV_HALO = 8
POOL_HALO = 16

VMEM_LIMIT_BYTES = 60 * 1024 * 1024

_IN_HBM = pl.BlockSpec(memory_space=pl.ANY)


def _inv_rms(x):
    return lax.rsqrt(jnp.mean(x * x, axis=-1, keepdims=True) + EPS)


def _rms_norm(x, g):
    return x * _inv_rms(x) * g


def _dot(a, b):
    return jnp.dot(a.astype(BF16), b.astype(BF16), preferred_element_type=F32)


def _pre_norm_split(x, g):
    return (x * g).astype(BF16), _inv_rms(x)


def _layer_resident(layer, shape):
    zeros = (0,) * len(shape)
    return pl.BlockSpec((None,) + tuple(shape), lambda i: (layer,) + zeros,
                        pipeline_mode=pl.Buffered(1))


def _whole_resident(shape):
    zeros = (0,) * len(shape)
    return pl.BlockSpec(tuple(shape), lambda i: zeros, pipeline_mode=pl.Buffered(1))


def _first_step_waits(tile):
    @pl.when(pl.program_id(0) == 0)
    def _():
        tile(True)

    @pl.when(pl.program_id(0) != 0)
    def _():
        tile(False)


def _ffn_kernel(layer, x_ref, pre_ref, post_ref, wgu_hbm, wd_hbm, o_ref,
                act_ref, wgu_ref, wd_ref, sems):
    n_chunks = D_FF // FF_CHUNK

    def gate_up_copy(which, c):
        cols = pl.ds(which * D_FF + c * FF_CHUNK, FF_CHUNK)
        return pltpu.make_async_copy(wgu_hbm.at[layer, :, cols], wgu_ref.at[:, cols],
                                     sems.at[which * n_chunks + c])

    def down_copy():
        return pltpu.make_async_copy(wd_hbm.at[layer], wd_ref, sems.at[2 * n_chunks])

    def tile(wait_for_weights):
        if wait_for_weights:
            for c in range(n_chunks):
                gate_up_copy(0, c).start()
                gate_up_copy(1, c).start()
            down_copy().start()
        x = x_ref[...]
        h, r = _pre_norm_split(x, pre_ref[layer:layer + 1, :])
        for c in range(n_chunks):
            c0 = c * FF_CHUNK
            if wait_for_weights:
                gate_up_copy(0, c).wait()
                gate_up_copy(1, c).wait()
            g = r * _dot(h, wgu_ref[:, c0:c0 + FF_CHUNK])
            u = r * _dot(h, wgu_ref[:, D_FF + c0:D_FF + c0 + FF_CHUNK])
            act_ref[:, c0:c0 + FF_CHUNK] = (g * jax.nn.sigmoid(g) * u).astype(BF16)
        if wait_for_weights:
            down_copy().wait()
        y = _dot(act_ref[...], wd_ref[...])
        o_ref[...] = x + _rms_norm(y, 0.5 * post_ref[layer:layer + 1, :])

    _first_step_waits(tile)


def _ffn(layer, x, pre, post, wgu, wd):
    t = x.shape[0]
    row_spec = pl.BlockSpec((FFN_TM, D_MODEL), lambda i: (i, 0))
    n_copies = 2 * (D_FF // FF_CHUNK) + 1
    return pl.pallas_call(
        functools.partial(_ffn_kernel, layer),
        grid=(t // FFN_TM,),
        in_specs=[
            row_spec,
            _whole_resident(pre.shape),
            _whole_resident(post.shape),
            _IN_HBM,
            _IN_HBM,
        ],
        out_specs=row_spec,
        out_shape=jax.ShapeDtypeStruct(x.shape, x.dtype),
        scratch_shapes=[
            pltpu.VMEM((FFN_TM, D_FF), BF16),
            pltpu.VMEM((D_MODEL, 2 * D_FF), F32),
            pltpu.VMEM((D_FF, D_MODEL), F32),
            pltpu.SemaphoreType.DMA((n_copies,)),
        ],
        compiler_params=pltpu.CompilerParams(
            dimension_semantics=("arbitrary",), vmem_limit_bytes=VMEM_LIMIT_BYTES),
        name="ffn",
    )(x, pre, post, wgu, wd)


_MIX_WIN_ORDER = (3, 1, 2, 0, 4, 5)
_MIX_WIN_BEFORE_POOL_GROUPS = 3


def _mixer_kernel(layer, x_ref, pre_ref, post_ref, win_hbm, bgate_ref, convw_ref, wco_hbm,
                  wpg_hbm, bpg_ref, pscale_ref, wpo_hbm, wo_hbm, o_ref,
                  cv_ref, p_ref, yp_ref, win_ref, wco_ref, wpg_ref, wpo_ref, wo_ref, sems):
    tm = MIX_TM
    tile_in_seq = lax.rem(pl.program_id(0), SEQ // tm)
    n_win = len(_MIX_WIN_ORDER)

    def win_copy(k):
        cols = pl.ds(k * D_MODEL, D_MODEL)
        return pltpu.make_async_copy(win_hbm.at[layer, :, cols], win_ref.at[:, cols], sems.at[k])

    def wpg_copy():
        return pltpu.make_async_copy(wpg_hbm.at[layer], wpg_ref, sems.at[n_win])

    def wco_copy():
        return pltpu.make_async_copy(wco_hbm.at[layer], wco_ref, sems.at[n_win + 1])

    def wpo_copy():
        return pltpu.make_async_copy(wpo_hbm.at[layer], wpo_ref, sems.at[n_win + 2])

    def wo_copy():
        return pltpu.make_async_copy(wo_hbm.at[layer], wo_ref, sems.at[n_win + 3])

    @pl.when(tile_in_seq == 0)
    def _():
        cv_ref[0:CONV_HALO, :] = jnp.zeros((CONV_HALO, D_MODEL), F32)
        p_ref[0:POOL_HALO, :] = jnp.zeros((POOL_HALO, D_MODEL), F32)

    @pl.when(tile_in_seq != 0)
    def _():
        cv_ref[0:CONV_HALO, :] = cv_ref[tm:tm + CONV_HALO, :]
        p_ref[0:POOL_HALO, :] = p_ref[tm:tm + POOL_HALO, :]

    def tile(wait_for_weights):
        def arrived(copy):
            if wait_for_weights:
                copy.wait()

        if wait_for_weights:
            for k in _MIX_WIN_ORDER[:_MIX_WIN_BEFORE_POOL_GROUPS]:
                win_copy(k).start()
            wpg_copy().start()
            for k in _MIX_WIN_ORDER[_MIX_WIN_BEFORE_POOL_GROUPS:]:
                win_copy(k).start()
            wco_copy().start()
            wpo_copy().start()
            wo_copy().start()

        x = x_ref[...]
        h, r = _pre_norm_split(x, pre_ref[layer:layer + 1, :])

        def proj(k):
            arrived(win_copy(k))
            return r * _dot(h, win_ref[:, k * D_MODEL:(k + 1) * D_MODEL])

        p_ref[POOL_HALO:POOL_HALO + tm, :] = proj(3)
        cv_ref[CONV_HALO:CONV_HALO + tm, :] = proj(1) * proj(2)

        arrived(wpg_copy())
        t_seq = tile_in_seq * tm + lax.broadcasted_iota(jnp.int32, (tm, 1), 0)
        for gi, win in enumerate(POOL_WINDOWS):
            cols = slice(gi * POOL_GC, (gi + 1) * POOL_GC)
            ext = p_ref[:, cols]
            s = ext
            k = 1
            while k < win:
                s = s + pltpu.roll(s, k, axis=0)
                k *= 2
            inv_count = 1.0 / jnp.minimum(t_seq + 1, win).astype(F32)
            d = s[POOL_HALO:, :] * inv_count - ext[POOL_HALO:, :]
            yg = _dot(d, wpg_ref[gi]) + bpg_ref[layer, gi:gi + 1, :]
            yp_ref[:, cols] = (yg * pscale_ref[layer:layer + 1, cols]).astype(BF16)

        gate_b = proj(0)
        cw = convw_ref[...]
        cv = cv_ref[...]
        y = (cw[0:1, :] * pltpu.roll(cv, 2, axis=0) + cw[1:2, :] * pltpu.roll(cv, 1, axis=0)
             + cw[2:3, :] * cv)
        a = (gate_b * y[CONV_HALO:, :]).astype(BF16)
        g_a = jax.nn.sigmoid(proj(4) + bgate_ref[layer:layer + 1, 0:D_MODEL])
        g_p = jax.nn.sigmoid(proj(5) + bgate_ref[layer:layer + 1, D_MODEL:2 * D_MODEL])
        arrived(wco_copy())
        y_a = _dot(a, wco_ref[...])
        arrived(wpo_copy())
        y_p = _dot(yp_ref[...], wpo_ref[...])
        mixed = g_a * y_a + g_p * y_p

        arrived(wo_copy())
        out = _dot(mixed, wo_ref[...])
        o_ref[...] = x + _rms_norm(out, post_ref[layer:layer + 1, :])

    _first_step_waits(tile)


def _mixer(layer, x, pre, post, win, bgate, convw, wco, wpg, bpg, pscale, wpo, wo):
    t = x.shape[0]
    tm = MIX_TM
    row_spec = pl.BlockSpec((tm, D_MODEL), lambda i: (i, 0))
    n_groups = len(POOL_WINDOWS)
    return pl.pallas_call(
        functools.partial(_mixer_kernel, layer),
        grid=(t // tm,),
        in_specs=[
            row_spec,
            _whole_resident(pre.shape),
            _whole_resident(post.shape),
            _IN_HBM,
            _whole_resident(bgate.shape),
            _layer_resident(layer, (CONV_K, D_MODEL)),
            _IN_HBM,
            _IN_HBM,
            _whole_resident(bpg.shape),
            _whole_resident(pscale.shape),
            _IN_HBM,
            _IN_HBM,
        ],
        out_specs=row_spec,
        out_shape=jax.ShapeDtypeStruct(x.shape, x.dtype),
        scratch_shapes=[
            pltpu.VMEM((CONV_HALO + tm, D_MODEL), F32),
            pltpu.VMEM((POOL_HALO + tm, D_MODEL), F32),
            pltpu.VMEM((tm, D_MODEL), BF16),
            pltpu.VMEM((D_MODEL, 6 * D_MODEL), F32),
            pltpu.VMEM((D_MODEL, D_MODEL), F32),
            pltpu.VMEM((n_groups, POOL_GC, POOL_GC), F32),
            pltpu.VMEM((D_MODEL, D_MODEL), F32),
            pltpu.VMEM((D_MODEL, D_MODEL), F32),
            pltpu.SemaphoreType.DMA((len(_MIX_WIN_ORDER) + 4,)),
        ],
        compiler_params=pltpu.CompilerParams(
            dimension_semantics=("arbitrary",), vmem_limit_bytes=VMEM_LIMIT_BYTES),
        name="mixer",
    )(x, pre, post, win, bgate, convw, wco, wpg, bpg, pscale, wpo, wo)


def kernel(x, ffn1_pre, ffn1_post, ffn1_w_gate_up, ffn1_w_down, mix_pre, mix_post, w_in, b_gate,
           conv_w, w_conv_out, w_pool_group, b_pool_group, pool_scale, w_pool_out, w_o,
           ffn2_pre, ffn2_post, ffn2_w_gate_up, ffn2_w_down):
    b, s, d = x.shape
    assert (s, d) == (SEQ, D_MODEL) and (b * s) % FFN_TM == 0 and SEQ % MIX_TM == 0
    depth = w_in.shape[0]
    xf = x.reshape(b * s, d)

    for l in range(depth):
        xf = _ffn(l, xf, ffn1_pre, ffn1_post, ffn1_w_gate_up, ffn1_w_down)
        xf = _mixer(l, xf, mix_pre, mix_post, w_in, b_gate, conv_w, w_conv_out,
                    w_pool_group, b_pool_group, pool_scale, w_pool_out, w_o)
        xf = _ffn(l, xf, ffn2_pre, ffn2_post, ffn2_w_gate_up, ffn2_w_down)
    return xf.reshape(b, s, d)
```

```python
import functools

import jax
import jax.numpy as jnp
from jax import lax
from jax.experimental import pallas as pl
from jax.experimental.pallas import tpu as pltpu

D_MODEL = 1024
SEQ = 2048
D_FF = 2816
CONV_K = 3
POOL_WINDOWS = (2, 4, 8, 16)
POOL_GC = D_MODEL // len(POOL_WINDOWS)
EPS = 1e-6

F32 = jnp.float32
BF16 = jnp.bfloat16

FFN_TM = 512
MIX_TM = 512
FF_CHUNK = 256
CONV_HALO = 8
POOL_HALO = 16

VMEM_LIMIT_BYTES = 60 * 1024 * 1024

_IN_HBM = pl.BlockSpec(memory_space=pl.ANY)


def _inv_rms(x):
    return lax.rsqrt(jnp.mean(x * x, axis=-1, keepdims=True) + EPS)


def _rms_norm(x, g):
    return x * _inv_rms(x) * g


def _dot(a, b):
    return jnp.dot(a.astype(BF16), b.astype(BF16), preferred_element_type=F32)


def _pre_norm_split(x, g):
    return (x * g).astype(BF16), _inv_rms(x)


def _layer_resident(layer, shape):
    zeros = (0,) * len(shape)
    return pl.BlockSpec((None,) + tuple(shape), lambda i: (layer,) + zeros,
                        pipeline_mode=pl.Buffered(1))


def _whole_resident(shape):
    zeros = (0,) * len(shape)
    return pl.BlockSpec(tuple(shape), lambda i: zeros, pipeline_mode=pl.Buffered(1))


def _first_step_waits(tile):
    @pl.when(pl.program_id(0) == 0)
    def _():
        tile(True)

    @pl.when(pl.program_id(0) != 0)
    def _():
        tile(False)


def _ffn_kernel(layer, x_ref, pre_ref, post_ref, wgu_hbm, wd_hbm, o_ref,
                act_ref, wgu_ref, wd_ref, sems):
    n_chunks = D_FF // FF_CHUNK

    def gate_up_copy(which, c):
        cols = pl.ds(which * D_FF + c * FF_CHUNK, FF_CHUNK)
        return pltpu.make_async_copy(wgu_hbm.at[layer, :, cols], wgu_ref.at[:, cols],
                                     sems.at[which * n_chunks + c])

    def down_copy():
        return pltpu.make_async_copy(wd_hbm.at[layer], wd_ref, sems.at[2 * n_chunks])

    def tile(wait_for_weights):
        if wait_for_weights:
            for c in range(n_chunks):
                gate_up_copy(0, c).start()
                gate_up_copy(1, c).start()
            down_copy().start()
        x = x_ref[...]
        h, r = _pre_norm_split(x, pre_ref[layer:layer + 1, :])
        for c in range(n_chunks):
            c0 = c * FF_CHUNK
            if wait_for_weights:
                gate_up_copy(0, c).wait()
                gate_up_copy(1, c).wait()
            g = r * _dot(h, wgu_ref[:, c0:c0 + FF_CHUNK])
            u = r * _dot(h, wgu_ref[:, D_FF + c0:D_FF + c0 + FF_CHUNK])
            act_ref[:, c0:c0 + FF_CHUNK] = (g * jax.nn.sigmoid(g) * u).astype(BF16)
        if wait_for_weights:
            down_copy().wait()
        y = _dot(act_ref[...], wd_ref[...])
        o_ref[...] = x + _rms_norm(y, 0.5 * post_ref[layer:layer + 1, :])

    _first_step_waits(tile)


def _ffn(layer, x, pre, post, wgu, wd):
    t = x.shape[0]
    row_spec = pl.BlockSpec((FFN_TM, D_MODEL), lambda i: (i, 0))
    n_copies = 2 * (D_FF // FF_CHUNK) + 1
    return pl.pallas_call(
        functools.partial(_ffn_kernel, layer),
        grid=(t // FFN_TM,),
        in_specs=[
            row_spec,
            _whole_resident(pre.shape),
            _whole_resident(post.shape),
            _IN_HBM,
            _IN_HBM,
        ],
        out_specs=row_spec,
        out_shape=jax.ShapeDtypeStruct(x.shape, x.dtype),
        scratch_shapes=[
            pltpu.VMEM((FFN_TM, D_FF), BF16),
            pltpu.VMEM((D_MODEL, 2 * D_FF), F32),
            pltpu.VMEM((D_FF, D_MODEL), F32),
            pltpu.SemaphoreType.DMA((n_copies,)),
        ],
        compiler_params=pltpu.CompilerParams(
            dimension_semantics=("arbitrary",), vmem_limit_bytes=VMEM_LIMIT_BYTES),
        name="ffn",
    )(x, pre, post, wgu, wd)


_MIX_WIN_ORDER = (3, 1, 2, 0, 4, 5)
_MIX_WIN_BEFORE_POOL_GROUPS = 3


def _mixer_kernel(layer, x_ref, pre_ref, post_ref, win_hbm, bgate_ref, convw_ref, wco_hbm,
                  wpg_hbm, bpg_ref, pscale_ref, wpo_hbm, wo_hbm, o_ref,
                  cv_ref, p_ref, d_ref, win_ref, wco_ref, wpg_ref, wpo_ref, wo_ref, cpool_ref,
                  sems):
    tm = MIX_TM
    tile_in_seq = lax.rem(pl.program_id(0), SEQ // tm)
    n_win = len(_MIX_WIN_ORDER)

    def win_copy(k):
        cols = pl.ds(k * D_MODEL, D_MODEL)
        return pltpu.make_async_copy(win_hbm.at[layer, :, cols], win_ref.at[:, cols], sems.at[k])

    def wpg_copy():
        return pltpu.make_async_copy(wpg_hbm.at[layer], wpg_ref, sems.at[n_win])

    def wco_copy():
        return pltpu.make_async_copy(wco_hbm.at[layer], wco_ref, sems.at[n_win + 1])

    def wpo_copy():
        return pltpu.make_async_copy(wpo_hbm.at[layer], wpo_ref, sems.at[n_win + 2])

    def wo_copy():
        return pltpu.make_async_copy(wo_hbm.at[layer], wo_ref, sems.at[n_win + 3])

    @pl.when(tile_in_seq == 0)
    def _():
        cv_ref[0:CONV_HALO, :] = jnp.zeros((CONV_HALO, D_MODEL), F32)
        p_ref[0:POOL_HALO, :] = jnp.zeros((POOL_HALO, D_MODEL), F32)

    @pl.when(tile_in_seq != 0)
    def _():
        cv_ref[0:CONV_HALO, :] = cv_ref[tm:tm + CONV_HALO, :]
        p_ref[0:POOL_HALO, :] = p_ref[tm:tm + POOL_HALO, :]

    def fold_pool_branch():
        bias = jnp.zeros((8, D_MODEL), F32)
        folded = []
        for gi in range(len(POOL_WINDOWS)):
            rows = slice(gi * POOL_GC, (gi + 1) * POOL_GC)
            scale = pscale_ref[layer:layer + 1, rows]
            w_out = wpo_ref[rows, :]
            scaled_bias = jnp.broadcast_to(bpg_ref[layer, gi:gi + 1, :] * scale, (8, POOL_GC))
            bias = bias + _dot(scaled_bias, w_out)
            folded.append(_dot(wpg_ref[gi] * scale, w_out))
        cpool_ref[...] = bias
        for gi, m in enumerate(folded):
            wpo_ref[gi * POOL_GC:(gi + 1) * POOL_GC, :] = m

    def tile(wait_for_weights):
        def arrived(copy):
            if wait_for_weights:
                copy.wait()

        if wait_for_weights:
            for k in _MIX_WIN_ORDER[:_MIX_WIN_BEFORE_POOL_GROUPS]:
                win_copy(k).start()
            wpg_copy().start()
            wpo_copy().start()
            for k in _MIX_WIN_ORDER[_MIX_WIN_BEFORE_POOL_GROUPS:]:
                win_copy(k).start()
            wco_copy().start()
            wo_copy().start()

        x = x_ref[...]
        h, r = _pre_norm_split(x, pre_ref[layer:layer + 1, :])

        def proj(k):
            arrived(win_copy(k))
            return r * _dot(h, win_ref[:, k * D_MODEL:(k + 1) * D_MODEL])

        p_ref[POOL_HALO:POOL_HALO + tm, :] = proj(3)
        cv_ref[CONV_HALO:CONV_HALO + tm, :] = proj(1) * proj(2)

        if wait_for_weights:
            wpg_copy().wait()
            wpo_copy().wait()
            fold_pool_branch()
        t_seq = tile_in_seq * tm + lax.broadcasted_iota(jnp.int32, (tm, 1), 0)
        for gi, win in enumerate(POOL_WINDOWS):
            cols = slice(gi * POOL_GC, (gi + 1) * POOL_GC)
            ext = p_ref[:, cols]
            s = ext
            k = 1
            while k < win:
                s = s + pltpu.roll(s, k, axis=0)
                k *= 2
            inv_count = 1.0 / jnp.minimum(t_seq + 1, win).astype(F32)
            d = s[POOL_HALO:, :] * inv_count - ext[POOL_HALO:, :]
            d_ref[:, cols] = d.astype(BF16)

        gate_b = proj(0)
        cw = convw_ref[...]
        cv = cv_ref[...]
        y = (cw[0:1, :] * pltpu.roll(cv, 2, axis=0) + cw[1:2, :] * pltpu.roll(cv, 1, axis=0)
             + cw[2:3, :] * cv)
        a = (gate_b * y[CONV_HALO:, :]).astype(BF16)
        g_a = jax.nn.sigmoid(proj(4) + bgate_ref[layer:layer + 1, 0:D_MODEL])
        g_p = jax.nn.sigmoid(proj(5) + bgate_ref[layer:layer + 1, D_MODEL:2 * D_MODEL])
        arrived(wco_copy())
        y_a = _dot(a, wco_ref[...])
        y_p = _dot(d_ref[...], wpo_ref[...]) + cpool_ref[0:1, :]
        mixed = g_a * y_a + g_p * y_p

        arrived(wo_copy())
        out = _dot(mixed, wo_ref[...])
        o_ref[...] = x + _rms_norm(out, post_ref[layer:layer + 1, :])

    _first_step_waits(tile)


def _mixer(layer, x, pre, post, win, bgate, convw, wco, wpg, bpg, pscale, wpo, wo):
    t = x.shape[0]
    tm = MIX_TM
    row_spec = pl.BlockSpec((tm, D_MODEL), lambda i: (i, 0))
    n_groups = len(POOL_WINDOWS)
    return pl.pallas_call(
        functools.partial(_mixer_kernel, layer),
        grid=(t // tm,),
        in_specs=[
            row_spec,
            _whole_resident(pre.shape),
            _whole_resident(post.shape),
            _IN_HBM,
            _whole_resident(bgate.shape),
            _layer_resident(layer, (CONV_K, D_MODEL)),
            _IN_HBM,
            _IN_HBM,
            _whole_resident(bpg.shape),
            _whole_resident(pscale.shape),
            _IN_HBM,
            _IN_HBM,
        ],
        out_specs=row_spec,
        out_shape=jax.ShapeDtypeStruct(x.shape, x.dtype),
        scratch_shapes=[
            pltpu.VMEM((CONV_HALO + tm, D_MODEL), F32),
            pltpu.VMEM((POOL_HALO + tm, D_MODEL), F32),
            pltpu.VMEM((tm, D_MODEL), BF16),
            pltpu.VMEM((D_MODEL, 6 * D_MODEL), F32),
            pltpu.VMEM((D_MODEL, D_MODEL), F32),
            pltpu.VMEM((n_groups, POOL_GC, POOL_GC), F32),
            pltpu.VMEM((D_MODEL, D_MODEL), F32),
            pltpu.VMEM((D_MODEL, D_MODEL), F32),
            pltpu.VMEM((8, D_MODEL), F32),
            pltpu.SemaphoreType.DMA((len(_MIX_WIN_ORDER) + 4,)),
        ],
        compiler_params=pltpu.CompilerParams(
            dimension_semantics=("arbitrary",), vmem_limit_bytes=VMEM_LIMIT_BYTES),
        name="mixer",
    )(x, pre, post, win, bgate, convw, wco, wpg, bpg, pscale, wpo, wo)


def kernel(x, ffn1_pre, ffn1_post, ffn1_w_gate_up, ffn1_w_down, mix_pre, mix_post, w_in, b_gate,
           conv_w, w_conv_out, w_pool_group, b_pool_group, pool_scale, w_pool_out, w_o,
           ffn2_pre, ffn2_post, ffn2_w_gate_up, ffn2_w_down):
    b, s, d = x.shape
    assert (s, d) == (SEQ, D_MODEL) and (b * s) % FFN_TM == 0 and SEQ % MIX_TM == 0
    depth = w_in.shape[0]
    xf = x.reshape(b * s, d)

    for l in range(depth):
        xf = _ffn(l, xf, ffn1_pre, ffn1_post, ffn1_w_gate_up, ffn1_w_down)
        xf = _mixer(l, xf, mix_pre, mix_post, w_in, b_gate, conv_w, w_conv_out,
                    w_pool_group, b_pool_group, pool_scale, w_pool_out, w_o)
        xf = _ffn(l, xf, ffn2_pre, ffn2_post, ffn2_w_gate_up, ffn2_w_down)
    return xf.reshape(b, s, d)
```

```python
import functools

import jax
import jax.numpy as jnp
from jax import lax
from jax.experimental import pallas as pl
from jax.experimental.pallas import tpu as pltpu

D_MODEL = 1024
SEQ = 2048
D_FF = 2816
CONV_K = 3
POOL_WINDOWS = (2, 4, 8, 16)
POOL_GC = D_MODEL // len(POOL_WINDOWS)
EPS = 1e-6

F32 = jnp.float32
BF16 = jnp.bfloat16

FFN_TM = 512
MIX_TM = 512
FF_CHUNK = 256
CONV_HALO = 8
POOL_HALO = 16

VMEM_LIMIT_BYTES = 60 * 1024 * 1024

_IN_HBM = pl.BlockSpec(memory_space=pl.ANY)


def _inv_rms(x):
    return lax.rsqrt(jnp.mean(x * x, axis=-1, keepdims=True) + EPS)


def _rms_norm(x, g):
    return x * _inv_rms(x) * g


def _dot(a, b):
    return jnp.dot(a.astype(BF16), b.astype(BF16), preferred_element_type=F32)


def _pre_norm_split(x, g):
    return (x * g).astype(BF16), _inv_rms(x)


def _layer_resident(layer, shape):
    zeros = (0,) * len(shape)
    return pl.BlockSpec((None,) + tuple(shape), lambda i: (layer,) + zeros,
                        pipeline_mode=pl.Buffered(1))


def _whole_resident(shape):
    zeros = (0,) * len(shape)
    return pl.BlockSpec(tuple(shape), lambda i: zeros, pipeline_mode=pl.Buffered(1))


def _first_step_waits(tile):
    @pl.when(pl.program_id(0) == 0)
    def _():
        tile(True)

    @pl.when(pl.program_id(0) != 0)
    def _():
        tile(False)


def _ffn_kernel(layer, x_ref, pre_ref, post_ref, wgu_hbm, wd_hbm, o_ref,
                act_ref, wgu_ref, wd_ref, sems):
    n_chunks = D_FF // FF_CHUNK

    def gate_up_copy(which, c):
        cols = pl.ds(which * D_FF + c * FF_CHUNK, FF_CHUNK)
        return pltpu.make_async_copy(wgu_hbm.at[layer, :, cols], wgu_ref.at[:, cols],
                                     sems.at[which * n_chunks + c])

    def down_copy():
        return pltpu.make_async_copy(wd_hbm.at[layer], wd_ref, sems.at[2 * n_chunks])

    def tile(wait_for_weights):
        if wait_for_weights:
            for c in range(n_chunks):
                gate_up_copy(0, c).start()
                gate_up_copy(1, c).start()
            down_copy().start()
        x = x_ref[...]
        h, r = _pre_norm_split(x, pre_ref[layer:layer + 1, :])
        for c in range(n_chunks):
            c0 = c * FF_CHUNK
            if wait_for_weights:
                gate_up_copy(0, c).wait()
                gate_up_copy(1, c).wait()
            g = r * _dot(h, wgu_ref[:, c0:c0 + FF_CHUNK])
            u = r * _dot(h, wgu_ref[:, D_FF + c0:D_FF + c0 + FF_CHUNK])
            act_ref[:, c0:c0 + FF_CHUNK] = (g * jax.nn.sigmoid(g) * u).astype(BF16)
        if wait_for_weights:
            down_copy().wait()
        y = _dot(act_ref[...], wd_ref[...])
        o_ref[...] = x + _rms_norm(y, 0.5 * post_ref[layer:layer + 1, :])

    _first_step_waits(tile)


def _ffn(layer, x, pre, post, wgu, wd):
    t = x.shape[0]
    row_spec = pl.BlockSpec((FFN_TM, D_MODEL), lambda i: (i, 0))
    n_copies = 2 * (D_FF // FF_CHUNK) + 1
    return pl.pallas_call(
        functools.partial(_ffn_kernel, layer),
        grid=(t // FFN_TM,),
        in_specs=[
            row_spec,
            _whole_resident(pre.shape),
            _whole_resident(post.shape),
            _IN_HBM,
            _IN_HBM,
        ],
        out_specs=row_spec,
        out_shape=jax.ShapeDtypeStruct(x.shape, x.dtype),
        scratch_shapes=[
            pltpu.VMEM((FFN_TM, D_FF), BF16),
            pltpu.VMEM((D_MODEL, 2 * D_FF), F32),
            pltpu.VMEM((D_FF, D_MODEL), F32),
            pltpu.SemaphoreType.DMA((n_copies,)),
        ],
        compiler_params=pltpu.CompilerParams(
            dimension_semantics=("arbitrary",), vmem_limit_bytes=VMEM_LIMIT_BYTES),
        name="ffn",
    )(x, pre, post, wgu, wd)


_MIX_WIN_ORDER = (3, 1, 2, 0, 4, 5)
_MIX_WIN_BEFORE_POOL_GROUPS = 3


def _mixer_kernel(layer, x_ref, pre_ref, post_ref, win_hbm, bgate_ref, convw_ref, wco_hbm,
                  wpg_hbm, bpg_ref, pscale_ref, wpo_hbm, wo_hbm, o_ref,
                  cv_ref, p_ref, d_ref, win_ref, wco_ref, wpg_ref, wpo_ref, wo_ref, sems):
    tm = MIX_TM
    tile_in_seq = lax.rem(pl.program_id(0), SEQ // tm)
    n_win = len(_MIX_WIN_ORDER)

    def win_copy(k):
        cols = pl.ds(k * D_MODEL, D_MODEL)
        return pltpu.make_async_copy(win_hbm.at[layer, :, cols], win_ref.at[:, cols], sems.at[k])

    def wpg_copy():
        return pltpu.make_async_copy(wpg_hbm.at[layer], wpg_ref, sems.at[n_win])

    def wco_copy():
        return pltpu.make_async_copy(wco_hbm.at[layer], wco_ref, sems.at[n_win + 1])

    def wpo_copy():
        return pltpu.make_async_copy(wpo_hbm.at[layer], wpo_ref, sems.at[n_win + 2])

    def wo_copy():
        return pltpu.make_async_copy(wo_hbm.at[layer], wo_ref, sems.at[n_win + 3])

    @pl.when(tile_in_seq == 0)
    def _():
        cv_ref[0:CONV_HALO, :] = jnp.zeros((CONV_HALO, D_MODEL), F32)
        p_ref[0:POOL_HALO, :] = jnp.zeros((POOL_HALO, D_MODEL), F32)

    @pl.when(tile_in_seq != 0)
    def _():
        cv_ref[0:CONV_HALO, :] = cv_ref[tm:tm + CONV_HALO, :]
        p_ref[0:POOL_HALO, :] = p_ref[tm:tm + POOL_HALO, :]

    def fold_pool_branch():
        bias = jnp.zeros((8, D_MODEL), F32)
        folded = []
        for gi in range(len(POOL_WINDOWS)):
            rows = slice(gi * POOL_GC, (gi + 1) * POOL_GC)
            scale = pscale_ref[layer:layer + 1, rows]
            w_out = wpo_ref[rows, :]
            scaled_bias = jnp.broadcast_to(bpg_ref[layer, gi:gi + 1, :] * scale, (8, POOL_GC))
            bias = bias + _dot(scaled_bias, w_out)
            folded.append(_dot(wpg_ref[gi] * scale, w_out))
        for gi, m in enumerate(folded):
            wpo_ref[gi * POOL_GC:(gi + 1) * POOL_GC, :] = m
            wpg_ref[gi, 0:8, :] = bias[:, gi * POOL_GC:(gi + 1) * POOL_GC]

    def tile(wait_for_weights):
        def arrived(copy):
            if wait_for_weights:
                copy.wait()

        if wait_for_weights:
            for k in _MIX_WIN_ORDER[:_MIX_WIN_BEFORE_POOL_GROUPS]:
                win_copy(k).start()
            wpg_copy().start()
            wpo_copy().start()
            for k in _MIX_WIN_ORDER[_MIX_WIN_BEFORE_POOL_GROUPS:]:
                win_copy(k).start()
            wco_copy().start()
            wo_copy().start()

        x = x_ref[...]
        h, r = _pre_norm_split(x, pre_ref[layer:layer + 1, :])

        def proj(k):
            arrived(win_copy(k))
            return r * _dot(h, win_ref[:, k * D_MODEL:(k + 1) * D_MODEL])

        p_ref[POOL_HALO:POOL_HALO + tm, :] = proj(3)
        cv_ref[CONV_HALO:CONV_HALO + tm, :] = proj(1) * proj(2)

        if wait_for_weights:
            wpg_copy().wait()
            wpo_copy().wait()
            fold_pool_branch()
        t_seq = tile_in_seq * tm + lax.broadcasted_iota(jnp.int32, (tm, 1), 0)
        for gi, win in enumerate(POOL_WINDOWS):
            cols = slice(gi * POOL_GC, (gi + 1) * POOL_GC)
            ext = p_ref[:, cols]
            s = ext
            k = 1
            while k < win:
                s = s + pltpu.roll(s, k, axis=0)
                k *= 2
            inv_count = 1.0 / jnp.minimum(t_seq + 1, win).astype(F32)
            d = s[POOL_HALO:, :] * inv_count - ext[POOL_HALO:, :]
            d_ref[:, cols] = d.astype(BF16)

        gate_b = proj(0)
        cw = convw_ref[...]
        cv = cv_ref[...]
        y = (cw[0:1, :] * pltpu.roll(cv, 2, axis=0) + cw[1:2, :] * pltpu.roll(cv, 1, axis=0)
             + cw[2:3, :] * cv)
        a = (gate_b * y[CONV_HALO:, :]).astype(BF16)
        g_a = jax.nn.sigmoid(proj(4) + bgate_ref[layer:layer + 1, 0:D_MODEL])
        g_p = jax.nn.sigmoid(proj(5) + bgate_ref[layer:layer + 1, D_MODEL:2 * D_MODEL])
        arrived(wco_copy())
        y_a = _dot(a, wco_ref[...])
        pool_bias = jnp.concatenate([wpg_ref[gi, 0:1, :] for gi in range(len(POOL_WINDOWS))], axis=1)
        y_p = _dot(d_ref[...], wpo_ref[...]) + pool_bias
        mixed = g_a * y_a + g_p * y_p

        arrived(wo_copy())
        out = _dot(mixed, wo_ref[...])
        o_ref[...] = x + _rms_norm(out, post_ref[layer:layer + 1, :])

    _first_step_waits(tile)


def _mixer(layer, x, pre, post, win, bgate, convw, wco, wpg, bpg, pscale, wpo, wo):
    t = x.shape[0]
    tm = MIX_TM
    row_spec = pl.BlockSpec((tm, D_MODEL), lambda i: (i, 0))
    n_groups = len(POOL_WINDOWS)
    return pl.pallas_call(
        functools.partial(_mixer_kernel, layer),
        grid=(t // tm,),
        in_specs=[
            row_spec,
            _whole_resident(pre.shape),
            _whole_resident(post.shape),
            _IN_HBM,
            _whole_resident(bgate.shape),
            _layer_resident(layer, (CONV_K, D_MODEL)),
            _IN_HBM,
            _IN_HBM,
            _whole_resident(bpg.shape),
            _whole_resident(pscale.shape),
            _IN_HBM,
            _IN_HBM,
        ],
        out_specs=row_spec,
        out_shape=jax.ShapeDtypeStruct(x.shape, x.dtype),
        scratch_shapes=[
            pltpu.VMEM((CONV_HALO + tm, D_MODEL), F32),
            pltpu.VMEM((POOL_HALO + tm, D_MODEL), F32),
            pltpu.VMEM((tm, D_MODEL), BF16),
            pltpu.VMEM((D_MODEL, 6 * D_MODEL), F32),
            pltpu.VMEM((D_MODEL, D_MODEL), F32),
            pltpu.VMEM((n_groups, POOL_GC, POOL_GC), F32),
            pltpu.VMEM((D_MODEL, D_MODEL), F32),
            pltpu.VMEM((D_MODEL, D_MODEL), F32),
            pltpu.SemaphoreType.DMA((len(_MIX_WIN_ORDER) + 4,)),
        ],
        compiler_params=pltpu.CompilerParams(
            dimension_semantics=("arbitrary",), vmem_limit_bytes=VMEM_LIMIT_BYTES),
        name="mixer",
    )(x, pre, post, win, bgate, convw, wco, wpg, bpg, pscale, wpo, wo)


def kernel(x, ffn1_pre, ffn1_post, ffn1_w_gate_up, ffn1_w_down, mix_pre, mix_post, w_in, b_gate,
           conv_w, w_conv_out, w_pool_group, b_pool_group, pool_scale, w_pool_out, w_o,
           ffn2_pre, ffn2_post, ffn2_w_gate_up, ffn2_w_down):
    b, s, d = x.shape
    assert (s, d) == (SEQ, D_MODEL) and (b * s) % FFN_TM == 0 and SEQ % MIX_TM == 0
    depth = w_in.shape[0]
    xf = x.reshape(b * s, d)

    for l in range(depth):
        xf = _ffn(l, xf, ffn1_pre, ffn1_post, ffn1_w_gate_up, ffn1_w_down)
        xf = _mixer(l, xf, mix_pre, mix_post, w_in, b_gate, conv_w, w_conv_out,
                    w_pool_group, b_pool_group, pool_scale, w_pool_out, w_o)
        xf = _ffn(l, xf, ffn2_pre, ffn2_post, ffn2_w_gate_up, ffn2_w_down)
    return xf.reshape(b, s, d)
```
